```python
import math
import jax
import jax.numpy as jnp
from jax import lax
import numpy as np

D_MODEL = 1024
BATCH = 8
SEQ = 2048
DEPTH = 4
DEC_BATCH = 128
DEC_SEQ = 8
PAST_LEN = 2048
PAGE_SIZE = 128

H_A = 4
D_QK = 64
D_VA = 2 * D_QK
QK_W = H_A * 2 * D_QK
ATT_W = H_A * D_VA
ROT_DIM = D_QK // 4
ROPE_THETA = 500000.0
Q_BLOCK = 128
CONV_CH = D_MODEL // 2
CONV_W = 31
H_C = 4
DK_C = D_MODEL // H_C
DV_C = 2 * DK_C
RET_THETA = 10000.0
RET_CHUNK = 128
N_EXPERTS = 16
N_GROUPS = 4
EXPERTS_PER_GROUP = N_EXPERTS // N_GROUPS
TOP_K = 2
D_FF = D_MODEL // 2
DN_ALPHA = (2 * DEPTH) ** 0.25
DN_BETA = (8 * DEPTH) ** -0.25
LN_EPS = 1e-5

N_A_LAYERS = (DEPTH + 1) // 2
N_C_LAYERS = DEPTH // 2
IN_A = 2 * QK_W + ATT_W + 2 * CONV_CH
IN_C = 2 * H_C * DK_C + 2 * H_C * DV_C

kernel_name = 'diffattn_conformer_retention_sharedrouter_moe_step'

F32 = jnp.float32


def _layer_norm(x, g, b):
    xf = x.astype(F32)
    mu = jnp.mean(xf, -1, keepdims=True)
    var = jnp.mean(jnp.square(xf - mu), -1, keepdims=True)
    return ((xf - mu) * lax.rsqrt(var + LN_EPS) * g.astype(F32) + b.astype(F32)).astype(x.dtype)


def _rms_norm(x, g):
    xf = x.astype(F32)
    return (xf * lax.rsqrt(jnp.mean(jnp.square(xf), -1, keepdims=True) + LN_EPS) * g.astype(F32)).astype(x.dtype)


def _head_norm(x):
    mu = jnp.mean(x, -1, keepdims=True)
    var = jnp.mean(jnp.square(x - mu), -1, keepdims=True)
    return (x - mu) * lax.rsqrt(var + LN_EPS)


def _rotate(t, pos, inv_freq):
    half = inv_freq.shape[0]
    ang = pos.astype(F32)[:, None] * inv_freq[None, :]
    shape = (1, ang.shape[0]) + (1,) * (t.ndim - 3) + (half,)
    cos = jnp.cos(ang).reshape(shape)
    sin = jnp.sin(ang).reshape(shape)
    t1 = t[..., :half].astype(F32)
    t2 = t[..., half:2 * half].astype(F32)
    rot = jnp.concatenate([t1 * cos - t2 * sin, t2 * cos + t1 * sin], -1).astype(t.dtype)
    return jnp.concatenate([rot, t[..., 2 * half:]], -1)


def _diff_attention(q, k, v, q_pos, k_pos, lam):
    scale = D_QK ** -0.5

    def block(args):
        qb, pb = args
        s = jnp.einsum('bqhmd,bkhmd->bhmqk', qb, k).astype(F32) * scale
        mask = k_pos[None, :] <= pb[:, None]
        p = jax.nn.softmax(jnp.where(mask, s, -jnp.inf), axis=-1)
        w = p[:, :, 0] - lam * p[:, :, 1]
        return jnp.einsum('bhqk,bkhe->bqhe', w.astype(v.dtype), v)

    b, lq = q.shape[0], q.shape[1]
    if lq > Q_BLOCK and lq % Q_BLOCK == 0:
        nb = lq // Q_BLOCK
        qb = jnp.moveaxis(q.reshape(b, nb, Q_BLOCK, H_A, 2, D_QK), 1, 0)
        pb = q_pos.reshape(nb, Q_BLOCK)
        out = lax.map(block, (qb, pb))
        return jnp.moveaxis(out, 0, 1).reshape(b, lq, H_A, D_VA)
    return block((q, q_pos))


def _diff_conv_mixer(x, pos, past, conv_buf, layer_idx, w_in, lam_p, subln_g, cw, cb, cg, cbeta, w_out):
    b, L, _ = x.shape
    q, k, v, a, g = jnp.split(x @ w_in, [QK_W, 2 * QK_W, 2 * QK_W + ATT_W, 2 * QK_W + ATT_W + CONV_CH], axis=-1)
    inv = ROPE_THETA ** (-jnp.arange(0, ROT_DIM, 2, dtype=F32) / ROT_DIM)
    q = _rotate(q.reshape(b, L, H_A, 2, D_QK), pos, inv)
    k = _rotate(k.reshape(b, L, H_A, 2, D_QK), pos, inv)
    v = v.reshape(b, L, H_A, D_VA)
    if past is None:
        k_all, v_all, k_pos = k, v, pos
    else:
        k_past, v_past = past
        k_all = jnp.concatenate([k_past.astype(k.dtype), k], 1)
        v_all = jnp.concatenate([v_past.astype(v.dtype), v], 1)
        k_pos = jnp.concatenate([jnp.arange(k_past.shape[1], dtype=pos.dtype), pos])
    lam_init = 0.8 - 0.6 * math.exp(-0.3 * layer_idx)
    lf = lam_p.astype(F32)
    lam = jnp.exp(jnp.sum(lf[0] * lf[1])) - jnp.exp(jnp.sum(lf[2] * lf[3])) + lam_init
    att = _rms_norm(_diff_attention(q, k_all, v_all, pos, k_pos, lam), subln_g) * (1.0 - lam_init)
    u = a * jax.nn.sigmoid(g)
    u_ext = jnp.concatenate([conv_buf.astype(u.dtype), u], 1)
    c = lax.conv_general_dilated(u_ext, cw[:, None, :].astype(u.dtype), window_strides=(1,), padding='VALID',
                                 dimension_numbers=('NWC', 'WIO', 'NWC'), feature_group_count=CONV_CH) + cb
    c = jax.nn.silu(_layer_norm(c, cg, cbeta))
    mix = jnp.concatenate([att.reshape(b, L, ATT_W), c], -1) @ w_out
    return mix, k.reshape(b, L, 2 * H_A, D_QK), v, u_ext[:, -(CONV_W - 1):]


def _retention(q, k, v, s0):
    b, L, h, _ = q.shape
    dv = v.shape[-1]
    c = RET_CHUNK if L % RET_CHUNK == 0 else L
    n = L // c
    log_g = jnp.log1p(-jnp.exp2(-5.0 - jnp.arange(h, dtype=F32)))
    idx = jnp.arange(c, dtype=F32)
    diff = idx[:, None] - idx[None, :]
    causal = diff >= 0
    d_intra = jnp.where(causal[None], jnp.exp(jnp.where(causal, diff, 0.0)[None] * log_g[:, None, None]), 0.0)
    q_dec = jnp.exp((idx[:, None] + 1.0) * log_g[None])
    k_dec = jnp.exp((c - 1.0 - idx[:, None]) * log_g[None])
    c_dec = jnp.exp(c * log_g)

    def to_chunks(t):
        return jnp.moveaxis(t.astype(F32).reshape(b, n, c, h, t.shape[-1]), 1, 0)

    def step(s, inp):
        qc, kc, vc = inp
        sc = jnp.einsum('bihd,bjhd->bhij', qc, kc) * d_intra
        o = jnp.einsum('bhij,bjhe->bihe', sc, vc) + jnp.einsum('bihd,bhde->bihe', qc * q_dec[:, :, None], s)
        s = s * c_dec[:, None, None] + jnp.einsum('bjhd,bjhe->bhde', kc * k_dec[:, :, None], vc)
        return s, o

    s, o = lax.scan(step, s0.astype(F32), (to_chunks(q), to_chunks(k), to_chunks(v)))
    return jnp.moveaxis(o, 0, 1).reshape(b, L, h, dv), s


def _retention_mixer(x, pos, s0, w_in, w_out):
    b, L, _ = x.shape
    q, k, v, g = jnp.split(x @ w_in, [H_C * DK_C, 2 * H_C * DK_C, 2 * H_C * DK_C + H_C * DV_C], axis=-1)
    inv = RET_THETA ** (-jnp.linspace(0.0, 1.0, DK_C // 2, dtype=F32))
    q = _rotate(q.reshape(b, L, H_C, DK_C), pos, inv)
    k = _rotate(k.reshape(b, L, H_C, DK_C), pos, inv) * (DK_C ** -0.5)
    o, s_new = _retention(q, k, v.reshape(b, L, H_C, DV_C), s0)
    o = _head_norm(o).astype(x.dtype).reshape(b, L, H_C * DV_C)
    return (jax.nn.silu(g) * o) @ w_out, s_new.astype(x.dtype)


def _moe(x, w_router, b_router, wg, wu, wd):
    xt = x.reshape(-1, D_MODEL)
    probs = jax.nn.softmax((xt @ w_router).astype(F32), axis=-1)
    sel = (probs + b_router.astype(F32)).reshape(-1, N_GROUPS, EXPERTS_PER_GROUP)
    group_score = lax.top_k(sel, TOP_K)[0].sum(-1)
    g_idx = jnp.argmax(group_score, axis=-1)
    in_group = jnp.take_along_axis(sel, g_idx[:, None, None], axis=1)[:, 0]
    _, local = lax.top_k(in_group, TOP_K)
    e_idx = g_idx[:, None] * EXPERTS_PER_GROUP + local
    w = jnp.take_along_axis(probs, e_idx, axis=-1)
    w = w / jnp.sum(w, -1, keepdims=True)
    gates = jnp.sum(jax.nn.one_hot(e_idx, N_EXPERTS, dtype=F32) * w[..., None], axis=1).astype(x.dtype)
    y = jnp.zeros_like(xt)
    for e in range(N_EXPERTS):
        hid = jax.nn.silu(xt @ wg[e]) * (xt @ wu[e])
        y = y + gates[:, e:e + 1] * (hid @ wd[e])
    return y.reshape(x.shape)


def _trunk(x, pos, paged, conv_state, ret_state, w_in_a, lambda_a, subln_a, conv_w, conv_b, conv_ln_g, conv_ln_b,
           w_out_a, w_in_c, w_out_c, ln_g, ln_b, w_router, b_router, w_e_gate, w_e_up, w_e_down):
    k_rows, v_rows, conv_rows, ret_rows = [], [], [], []
    for l in range(DEPTH):
        if l % 2 == 0:
            ia = l // 2
            past = None
            if paged is not None:
                cache_k, cache_v, page_table = paged
                nb, n_pages = page_table.shape
                past = (cache_k[ia][page_table].reshape(nb, n_pages * PAGE_SIZE, H_A, 2, D_QK),
                        cache_v[ia][page_table].reshape(nb, n_pages * PAGE_SIZE, H_A, D_VA))
            mix, k_new, v_new, buf = _diff_conv_mixer(x, pos, past, conv_state[ia], l, w_in_a[ia], lambda_a[ia],
                                                      subln_a[ia], conv_w[ia], conv_b[ia], conv_ln_g[ia],
                                                      conv_ln_b[ia], w_out_a[ia])
            k_rows.append(k_new)
            v_rows.append(v_new)
            conv_rows.append(buf)
        else:
            ic = l // 2
            mix, s_new = _retention_mixer(x, pos, ret_state[ic], w_in_c[ic], w_out_c[ic])
            ret_rows.append(s_new)
        x = _layer_norm(DN_ALPHA * x + mix, ln_g[l, 0], ln_b[l, 0])
        x = _layer_norm(DN_ALPHA * x + _moe(x, w_router, b_router, w_e_gate[l], w_e_up[l], w_e_down[l]),
                        ln_g[l, 1], ln_b[l, 1])
    return x, jnp.stack(k_rows), jnp.stack(v_rows), jnp.stack(conv_rows), jnp.stack(ret_rows)


def setup_inputs(seed: int = 0) -> dict:
    key = jax.random.key(seed)
    ks = jax.random.split(key, 24)
    n_pages = PAST_LEN // PAGE_SIZE
    n_phys = (DEC_BATCH * n_pages * 5) // 4

    def nrm(k, shape, s):
        return jax.random.normal(k, shape, F32) * s

    return {
        'x_prompt': nrm(ks[0], (BATCH, SEQ, D_MODEL), 1.0),
        'x_sample': nrm(ks[1], (DEC_BATCH, DEC_SEQ, D_MODEL), 1.0),
        'cache_k': nrm(ks[2], (N_A_LAYERS, n_phys, PAGE_SIZE, 2 * H_A, D_QK), 1.0),
        'cache_v': nrm(ks[3], (N_A_LAYERS, n_phys, PAGE_SIZE, H_A, D_VA), 1.0),
        'page_table': jax.random.permutation(ks[4], n_phys)[:DEC_BATCH * n_pages].reshape(DEC_BATCH, n_pages).astype(jnp.int32),
        'state_conv': nrm(ks[5], (N_A_LAYERS, DEC_BATCH, CONV_W - 1, CONV_CH), 0.5),
        'state_ret': nrm(ks[6], (N_C_LAYERS, DEC_BATCH, H_C, DK_C, DV_C), 0.5),
        'w_in_a': nrm(ks[7], (N_A_LAYERS, D_MODEL, IN_A), D_MODEL ** -0.5),
        'lambda_a': nrm(ks[8], (N_A_LAYERS, 4, D_QK), 0.1),
        'subln_a': 1.0 + nrm(ks[9], (N_A_LAYERS, D_VA), 0.01),
        'conv_w': nrm(ks[10], (N_A_LAYERS, CONV_W, CONV_CH), CONV_W ** -0.5),
        'conv_b': nrm(ks[11], (N_A_LAYERS, CONV_CH), 0.01),
        'conv_ln_g': 1.0 + nrm(ks[12], (N_A_LAYERS, CONV_CH), 0.01),
        'conv_ln_b': nrm(ks[13], (N_A_LAYERS, CONV_CH), 0.01),
        'w_out_a': nrm(ks[14], (N_A_LAYERS, ATT_W + CONV_CH, D_MODEL), DN_BETA * (ATT_W + CONV_CH) ** -0.5),
        'w_in_c': nrm(ks[15], (N_C_LAYERS, D_MODEL, IN_C), D_MODEL ** -0.5),
        'w_out_c': nrm(ks[16], (N_C_LAYERS, H_C * DV_C, D_MODEL), DN_BETA * (H_C * DV_C) ** -0.5),
        'ln_g': 1.0 + nrm(ks[17], (DEPTH, 2, D_MODEL), 0.01),
        'ln_b': nrm(ks[18], (DEPTH, 2, D_MODEL), 0.01),
        'w_router': nrm(ks[19], (D_MODEL, N_EXPERTS), D_MODEL ** -0.5),
        'b_router': nrm(ks[20], (N_EXPERTS,), 0.01),
        'w_e_gate': nrm(ks[21], (DEPTH, N_EXPERTS, D_MODEL, D_FF), D_MODEL ** -0.5),
        'w_e_up': nrm(ks[22], (DEPTH, N_EXPERTS, D_MODEL, D_FF), D_MODEL ** -0.5),
        'w_e_down': nrm(ks[23], (DEPTH, N_EXPERTS, D_FF, D_MODEL), DN_BETA * D_FF ** -0.5),
    }


def reference(x_prompt, x_sample, cache_k, cache_v, page_table, state_conv, state_ret,
              w_in_a, lambda_a, subln_a, conv_w, conv_b, conv_ln_g, conv_ln_b, w_out_a,
              w_in_c, w_out_c, ln_g, ln_b, w_router, b_router, w_e_gate, w_e_up, w_e_down):
    bp, seq_p = x_prompt.shape[0], x_prompt.shape[1]
    pos_p = jnp.arange(seq_p, dtype=jnp.int32)
    past_len = page_table.shape[1] * PAGE_SIZE
    pos_s = past_len + jnp.arange(x_sample.shape[1], dtype=jnp.int32)
    zeros_conv = jnp.zeros((N_A_LAYERS, bp, CONV_W - 1, CONV_CH), x_prompt.dtype)
    zeros_ret = jnp.zeros((N_C_LAYERS, bp, H_C, DK_C, DV_C), F32)
    y_prompt, k_p, v_p, conv_p, ret_p = _trunk(
        x_prompt, pos_p, None, zeros_conv, zeros_ret, w_in_a, lambda_a, subln_a, conv_w, conv_b, conv_ln_g,
        conv_ln_b, w_out_a, w_in_c, w_out_c, ln_g, ln_b, w_router, b_router, w_e_gate, w_e_up, w_e_down)
    y_sample, k_s, v_s, conv_s, ret_s = _trunk(
        x_sample, pos_s, (cache_k, cache_v, page_table), state_conv, state_ret, w_in_a, lambda_a, subln_a,
        conv_w, conv_b, conv_ln_g, conv_ln_b, w_out_a, w_in_c, w_out_c, ln_g, ln_b, w_router, b_router,
        w_e_gate, w_e_up, w_e_down)
    return (y_prompt, y_sample, k_p, v_p, conv_p, ret_p, k_s, v_s, conv_s, ret_s)
```

```python
import functools
import math

import jax
import jax.numpy as jnp
from jax import lax
from jax.experimental import pallas as pl
from jax.experimental.pallas import tpu as pltpu

F32 = jnp.float32
BF16 = jnp.bfloat16

D_MODEL = 1024
DEPTH = 4
PAGE_SIZE = 128
H_A = 4
D_QK = 64
D_VA = 2 * D_QK
QK_W = H_A * 2 * D_QK
ATT_W = H_A * D_VA
ROT_DIM = D_QK // 4
ROPE_THETA = 500000.0
CONV_CH = D_MODEL // 2
CONV_W = 31
H_C = 4
DK_C = D_MODEL // H_C
DV_C = 2 * DK_C
RET_THETA = 10000.0
N_EXPERTS = 16
N_GROUPS = 4
EXPERTS_PER_GROUP = N_EXPERTS // N_GROUPS
D_FF = D_MODEL // 2
DN_ALPHA = (2 * DEPTH) ** 0.25
LN_EPS = 1e-5
IN_A = 2 * QK_W + ATT_W + 2 * CONV_CH
IN_C = 2 * H_C * DK_C + 2 * H_C * DV_C

LANES = 128
SUBLANES = 8
CONV_HALO = 32
VMEM_LIMIT = 56 * 1024 * 1024


def _cparams(sem):
    return pltpu.CompilerParams(dimension_semantics=sem, vmem_limit_bytes=VMEM_LIMIT)


def _bdot(a, b):
    return jnp.dot(a.astype(BF16), b.astype(BF16), preferred_element_type=F32)


def _bdot_nt(a, b):
    return lax.dot_general(a.astype(BF16), b.astype(BF16), (((1,), (1,)), ((), ())),
                           preferred_element_type=F32)


def _layer_norm_rows(v, g, b):
    mu = jnp.mean(v, -1, keepdims=True)
    d = v - mu
    var = jnp.mean(d * d, -1, keepdims=True)
    return d * lax.rsqrt(var + LN_EPS) * g + b


def _silu(v):
    return v * (1.0 / (1.0 + jnp.exp(-v)))


def _sigmoid(v):
    return 1.0 / (1.0 + jnp.exp(-v))


def _inproj_a_kernel(x_ref, w_ref, cos_ref, sa_ref, sb_ref, o_ref):
    j = pl.program_id(1)
    y = _bdot(x_ref[...], w_ref[...])

    @pl.when(j < 2)
    def _():
        cos = cos_ref[...]
        sa = sa_ref[...]
        sb = sb_ref[...]
        for blk in range(QK_W // LANES):
            t = y[:, blk * LANES:(blk + 1) * LANES]
            r = (t * cos + pltpu.roll(t, ROT_DIM // 2, 1) * sa
                 + pltpu.roll(t, LANES - ROT_DIM // 2, 1) * sb)
            o_ref[:, blk * LANES:(blk + 1) * LANES] = r

    @pl.when(j >= 2)
    def _():
        o_ref[...] = y


def _rot_tables_a(pos):
    half = ROT_DIM // 2
    inv = ROPE_THETA ** (-jnp.arange(0, ROT_DIM, 2, dtype=F32) / ROT_DIM)
    ang = pos.astype(F32)[:, None] * inv[None, :]
    c, s = jnp.cos(ang), jnp.sin(ang)
    n = pos.shape[0]
    one = jnp.ones((n, D_QK - ROT_DIM), F32)
    zero = jnp.zeros((n, D_QK - ROT_DIM), F32)
    zh = jnp.zeros((n, half), F32)
    cos64 = jnp.concatenate([c, c, one], 1)
    sa64 = jnp.concatenate([zh, s, zero], 1)
    sb64 = jnp.concatenate([-s, zh, zero], 1)
    rep = LANES // D_QK
    return jnp.tile(cos64, (1, rep)), jnp.tile(sa64, (1, rep)), jnp.tile(sb64, (1, rep))


def _inproj_c_kernel(x_ref, w_ref, cos_ref, sin_ref, o_ref):
    j = pl.program_id(1)
    y = _bdot(x_ref[...], w_ref[...])

    @pl.when(j < 4)
    def _():
        cos = cos_ref[...]
        sin = sin_ref[...]
        scale = jnp.where(j >= 2, DK_C ** -0.5, 1.0).astype(F32)
        half = DK_C // 2
        for hd in range(2):
            t1 = y[:, hd * DK_C:hd * DK_C + half]
            t2 = y[:, hd * DK_C + half:(hd + 1) * DK_C]
            o_ref[:, hd * DK_C:hd * DK_C + half] = (t1 * cos - t2 * sin) * scale
            o_ref[:, hd * DK_C + half:(hd + 1) * DK_C] = (t2 * cos + t1 * sin) * scale

    @pl.when(j >= 4)
    def _():
        o_ref[...] = y


def _rot_tables_c(pos):
    inv = RET_THETA ** (-jnp.linspace(0.0, 1.0, DK_C // 2, dtype=F32))
    ang = pos.astype(F32)[:, None] * inv[None, :]
    return jnp.cos(ang), jnp.sin(ang)


def _inproj(x, w, li, tables, kern, tm, seq_tiles, n_prompt_tiles, name):
    t_all, d = x.shape
    n = w.shape[2]
    tn = 512
    tw = tables[0].shape[1]

    def tab_map(i, j):
        return (jnp.where(i < n_prompt_tiles, i % seq_tiles, seq_tiles), 0)

    return pl.pallas_call(
        kern,
        out_shape=jax.ShapeDtypeStruct((t_all, n), F32),
        grid=(t_all // tm, n // tn),
        in_specs=[pl.BlockSpec((tm, d), lambda i, j: (i, 0)),
                  pl.BlockSpec((None, d, tn), lambda i, j: (li, 0, j))]
                 + [pl.BlockSpec((tm, tw), tab_map) for _ in tables],
        out_specs=pl.BlockSpec((tm, tn), lambda i, j: (i, j)),
        compiler_params=_cparams(("parallel", "arbitrary")),
        name=name,
    )(x, w, *tables)


def _lambda_value(lam_ref, lam_init):
    lf = lam_ref[...]
    a = jnp.sum(lf[0:1, :] * lf[1:2, :], axis=-1, keepdims=True)
    b = jnp.sum(lf[2:3, :] * lf[3:4, :], axis=-1, keepdims=True)
    return jnp.exp(a) - jnp.exp(b) + lam_init


def _sub_norm(o, g, lam_init):
    ms = jnp.mean(o * o, -1, keepdims=True)
    return o * lax.rsqrt(ms + LN_EPS) * g * (1.0 - lam_init)


def _attn_prompt_kernel(q_ref, k_ref, v_ref, lam_ref, g_ref, o_ref, m_scr, l_scr, acc_scr, *, lam_init, tq):
    qi = pl.program_id(2)
    ki = pl.program_id(3)
    nk = pl.num_programs(3)

    @pl.when(ki == 0)
    def _():
        m_scr[...] = jnp.full(m_scr.shape, -jnp.inf, F32)
        l_scr[...] = jnp.zeros(l_scr.shape, F32)
        acc_scr[...] = jnp.zeros(acc_scr.shape, F32)

    @pl.when(ki <= qi)
    def _():
        q = q_ref[...] * (D_QK ** -0.5)
        k = k_ref[...]
        vb = v_ref[...].astype(BF16)
        row = qi * tq + lax.broadcasted_iota(jnp.int32, (tq, tq), 0)
        col = ki * tq + lax.broadcasted_iota(jnp.int32, (tq, tq), 1)
        visible = col <= row
        for m in range(2):
            s = _bdot_nt(q[:, m * D_QK:(m + 1) * D_QK], k[:, m * D_QK:(m + 1) * D_QK])
            s = jnp.where(visible, s, -jnp.inf)
            m_prev = m_scr[m]
            m_new = jnp.maximum(m_prev, jnp.max(s, -1, keepdims=True))
            alpha = jnp.exp(m_prev - m_new)
            p = jnp.exp(s - m_new)
            l_scr[m] = alpha * l_scr[m] + jnp.sum(p, -1, keepdims=True)
            acc_scr[m] = alpha * acc_scr[m] + jnp.dot(p.astype(BF16), vb, preferred_element_type=F32)
            m_scr[m] = m_new

    @pl.when(ki == nk - 1)
    def _():
        lam = _lambda_value(lam_ref, lam_init)
        o = acc_scr[0] / l_scr[0] - lam * (acc_scr[1] / l_scr[1])
        o_ref[...] = _sub_norm(o, g_ref[...], lam_init)


def _attn_prompt(qkvag, lam_p, subln_g, bsz, seq, lam_init, tq):
    nq = seq // tq
    kern = functools.partial(_attn_prompt_kernel, lam_init=lam_init, tq=tq)
    kcol = QK_W // D_VA
    vcol = 2 * QK_W // D_VA
    return pl.pallas_call(
        kern,
        out_shape=jax.ShapeDtypeStruct((bsz * seq, ATT_W), F32),
        grid=(bsz, H_A, nq, nq),
        in_specs=[pl.BlockSpec((tq, D_VA), lambda b, h, qi, ki: (b * nq + qi, h)),
                  pl.BlockSpec((tq, D_VA), lambda b, h, qi, ki: (b * nq + jnp.minimum(ki, qi), kcol + h)),
                  pl.BlockSpec((tq, D_VA), lambda b, h, qi, ki: (b * nq + jnp.minimum(ki, qi), vcol + h)),
                  pl.BlockSpec((4, D_QK), lambda b, h, qi, ki: (0, 0)),
                  pl.BlockSpec((1, D_VA), lambda b, h, qi, ki: (0, 0))],
        out_specs=pl.BlockSpec((tq, D_VA), lambda b, h, qi, ki: (b * nq + qi, h)),
        scratch_shapes=[pltpu.VMEM((2, tq, 1), F32), pltpu.VMEM((2, tq, 1), F32),
                        pltpu.VMEM((2, tq, D_VA), F32)],
        compiler_params=_cparams(("parallel", "parallel", "parallel", "arbitrary")),
        name="attn_prompt",
    )(qkvag, qkvag, qkvag, lam_p, subln_g.reshape(1, D_VA))


def _attn_sample_kernel(pt_ref, q_ref, kn_ref, vn_ref, lam_ref, g_ref, *rest, lam_init, n_pages, dec_seq):
    k_refs = rest[:n_pages]
    v_refs = rest[n_pages:2 * n_pages]
    o_ref = rest[2 * n_pages]
    nrow = 2 * H_A * dec_seq
    q = q_ref[...] * (D_QK ** -0.5)
    qt = jnp.concatenate([q] * (2 * H_A), axis=0)
    rid = lax.broadcasted_iota(jnp.int32, (nrow, QK_W), 0)
    cid = lax.broadcasted_iota(jnp.int32, (nrow, QK_W), 1)
    r_map = rid // (H_A * dec_seq)
    r_head = (rid // dec_seq) % H_A
    qbd = jnp.where(cid // D_QK == 2 * r_head + r_map, qt, 0.0).astype(BF16)

    s_past = [lax.dot_general(qbd, kr[...].astype(BF16), (((1,), (1,)), ((), ())),
                              preferred_element_type=F32) for kr in k_refs]
    s_new = lax.dot_general(qbd, kn_ref[...].astype(BF16), (((1,), (1,)), ((), ())),
                            preferred_element_type=F32)
    qpos = lax.broadcasted_iota(jnp.int32, (nrow, dec_seq), 0) % dec_seq
    kpos = lax.broadcasted_iota(jnp.int32, (nrow, dec_seq), 1)
    s_new = jnp.where(kpos <= qpos, s_new, -jnp.inf)
    m = jnp.max(s_new, -1, keepdims=True)
    for s in s_past:
        m = jnp.maximum(m, jnp.max(s, -1, keepdims=True))
    p_new = jnp.exp(s_new - m)
    l = jnp.sum(p_new, -1, keepdims=True)
    acc = jnp.dot(p_new.astype(BF16), vn_ref[...].astype(BF16), preferred_element_type=F32)
    for s, vr in zip(s_past, v_refs):
        p = jnp.exp(s - m)
        l = l + jnp.sum(p, -1, keepdims=True)
        acc = acc + jnp.dot(p.astype(BF16), vr[...].astype(BF16), preferred_element_type=F32)
    acc = acc / l
    half = H_A * dec_seq
    lam = _lambda_value(lam_ref, lam_init)
    diff = acc[:half] - lam * acc[half:]
    g = g_ref[...]
    for h in range(H_A):
        o = diff[h * dec_seq:(h + 1) * dec_seq, h * D_VA:(h + 1) * D_VA]
        o_ref[:, h * D_VA:(h + 1) * D_VA] = _sub_norm(o, g, lam_init)


def _attn_sample(qkvag, cache_k, cache_v, ia, page_table, lam_p, subln_g, row0, dec_b, dec_seq, lam_init):
    n_pages = page_table.shape[1]
    kern = functools.partial(_attn_sample_kernel, lam_init=lam_init, n_pages=n_pages, dec_seq=dec_seq)
    blk0 = row0 // dec_seq

    def page_spec(p):
        return pl.BlockSpec((None, None, PAGE_SIZE, QK_W), lambda s, pt: (ia, pt[s, p], 0, 0))

    grid_spec = pltpu.PrefetchScalarGridSpec(
        num_scalar_prefetch=1,
        grid=(dec_b,),
        in_specs=[pl.BlockSpec((dec_seq, QK_W), lambda s, pt: (blk0 + s, 0)),
                  pl.BlockSpec((dec_seq, QK_W), lambda s, pt: (blk0 + s, 1)),
                  pl.BlockSpec((dec_seq, ATT_W), lambda s, pt: (blk0 + s, 2)),
                  pl.BlockSpec((4, D_QK), lambda s, pt: (0, 0)),
                  pl.BlockSpec((1, D_VA), lambda s, pt: (0, 0))]
                 + [page_spec(p) for p in range(n_pages)]
                 + [page_spec(p) for p in range(n_pages)],
        out_specs=pl.BlockSpec((dec_seq, ATT_W), lambda s, pt: (s, 0)),
    )
    return pl.pallas_call(
        kern,
        out_shape=jax.ShapeDtypeStruct((dec_b * dec_seq, ATT_W), F32),
        grid_spec=grid_spec,
        compiler_params=_cparams(("arbitrary",)),
        name="attn_sample",
    )(page_table, qkvag, qkvag, qkvag, lam_p, subln_g.reshape(1, D_VA),
      *([cache_k] * n_pages), *([cache_v] * n_pages))


def _conv_taps(scr, cw_ref, row0, rows):
    acc = None
    for w in range(CONV_W):
        term = scr[pl.ds(row0 + w, rows), :] * cw_ref[w:w + 1, :]
        acc = term if acc is None else acc + term
    return acc


def _conv_prompt_kernel(a_ref, g_ref, ah_ref, gh_ref, cw_ref, cb_ref, lg_ref, lb_ref,
                        c_ref, st_ref, scr, *, ts, chunk):
    i = pl.program_id(1)
    n = pl.num_programs(1)
    u = a_ref[...] * _sigmoid(g_ref[...])
    uh = ah_ref[...] * _sigmoid(gh_ref[...])
    uh = jnp.where(i > 0, uh, 0.0)
    scr[0:CONV_HALO, :] = uh
    scr[CONV_HALO:CONV_HALO + ts, :] = u
    off = CONV_HALO - (CONV_W - 1)
    for c0 in range(0, ts, chunk):
        c = _conv_taps(scr, cw_ref, c0 + off, chunk) + cb_ref[...]
        c_ref[c0:c0 + chunk, :] = _silu(_layer_norm_rows(c, lg_ref[...], lb_ref[...]))

    @pl.when(i == n - 1)
    def _():
        st_ref[...] = scr[CONV_HALO + ts - (CONV_W - 1):CONV_HALO + ts, :]


def _conv_prompt(qkvag, cw, cb, lg, lb, bsz, seq, ts):
    ns = seq // ts
    hb = ts // CONV_HALO
    acol = (2 * QK_W + ATT_W) // CONV_CH
    kern = functools.partial(_conv_prompt_kernel, ts=ts, chunk=32)
    vec = lambda b, i: (0, 0)
    return pl.pallas_call(
        kern,
        out_shape=(jax.ShapeDtypeStruct((bsz * seq, CONV_CH), F32),
                   jax.ShapeDtypeStruct((bsz, CONV_W - 1, CONV_CH), F32)),
        grid=(bsz, ns),
        in_specs=[pl.BlockSpec((ts, CONV_CH), lambda b, i: (b * ns + i, acol)),
                  pl.BlockSpec((ts, CONV_CH), lambda b, i: (b * ns + i, acol + 1)),
                  pl.BlockSpec((CONV_HALO, CONV_CH), lambda b, i: (jnp.maximum((b * ns + i) * hb - 1, 0), acol)),
                  pl.BlockSpec((CONV_HALO, CONV_CH), lambda b, i: (jnp.maximum((b * ns + i) * hb - 1, 0), acol + 1)),
                  pl.BlockSpec((CONV_W, CONV_CH), vec),
                  pl.BlockSpec((1, CONV_CH), vec),
                  pl.BlockSpec((1, CONV_CH), vec),
                  pl.BlockSpec((1, CONV_CH), vec)],
        out_specs=(pl.BlockSpec((ts, CONV_CH), lambda b, i: (b * ns + i, 0)),
                   pl.BlockSpec((None, CONV_W - 1, CONV_CH), lambda b, i: (b, 0, 0))),
        scratch_shapes=[pltpu.VMEM((CONV_HALO + ts, CONV_CH), F32)],
        compiler_params=_cparams(("parallel", "arbitrary")),
        name="conv_prompt",
    )(qkvag, qkvag, qkvag, qkvag, cw, cb.reshape(1, -1), lg.reshape(1, -1), lb.reshape(1, -1))


def _conv_sample_kernel(a_ref, g_ref, st_ref, cw_ref, cb_ref, lg_ref, lb_ref, c_ref, so_ref, scr, *, nb, dec_seq):
    hist = CONV_W - 1
    u = a_ref[...] * _sigmoid(g_ref[...])
    for s in range(nb):
        scr[0:hist, :] = st_ref[s]
        scr[hist:hist + dec_seq, :] = u[s * dec_seq:(s + 1) * dec_seq, :]
        c = _conv_taps(scr, cw_ref, 0, dec_seq) + cb_ref[...]
        c_ref[s * dec_seq:(s + 1) * dec_seq, :] = _silu(_layer_norm_rows(c, lg_ref[...], lb_ref[...]))
        so_ref[s] = scr[dec_seq:dec_seq + hist, :]


def _conv_sample(qkvag, state, ia, cw, cb, lg, lb, row0, dec_b, dec_seq, nb):
    acol = (2 * QK_W + ATT_W) // CONV_CH
    blk0 = row0 // (nb * dec_seq)
    kern = functools.partial(_conv_sample_kernel, nb=nb, dec_seq=dec_seq)
    vec = lambda i: (0, 0)
    return pl.pallas_call(
        kern,
        out_shape=(jax.ShapeDtypeStruct((dec_b * dec_seq, CONV_CH), F32),
                   jax.ShapeDtypeStruct((dec_b, CONV_W - 1, CONV_CH), F32)),
        grid=(dec_b // nb,),
        in_specs=[pl.BlockSpec((nb * dec_seq, CONV_CH), lambda i: (blk0 + i, acol)),
                  pl.BlockSpec((nb * dec_seq, CONV_CH), lambda i: (blk0 + i, acol + 1)),
                  pl.BlockSpec((None, nb, CONV_W - 1, CONV_CH), lambda i: (ia, i, 0, 0)),
                  pl.BlockSpec((CONV_W, CONV_CH), vec),
                  pl.BlockSpec((1, CONV_CH), vec),
                  pl.BlockSpec((1, CONV_CH), vec),
                  pl.BlockSpec((1, CONV_CH), vec)],
        out_specs=(pl.BlockSpec((nb * dec_seq, CONV_CH), lambda i: (i, 0)),
                   pl.BlockSpec((nb, CONV_W - 1, CONV_CH), lambda i: (i, 0, 0))),
        scratch_shapes=[pltpu.VMEM((CONV_W - 1 + dec_seq + SUBLANES, CONV_CH), F32)],
        compiler_params=_cparams(("arbitrary",)),
        name="conv_sample",
    )(qkvag, qkvag, state, cw, cb.reshape(1, -1), lg.reshape(1, -1), lb.reshape(1, -1))


def _ret_tables(c):
    log_g = jnp.log1p(-jnp.exp2(-5.0 - jnp.arange(H_C, dtype=F32)))
    idx = jnp.arange(c, dtype=F32)
    diff = idx[:, None] - idx[None, :]
    causal = diff >= 0
    d_intra = jnp.where(causal[None], jnp.exp(jnp.where(causal, diff, 0.0)[None] * log_g[:, None, None]), 0.0)
    q_dec = jnp.exp((idx[None, :] + 1.0) * log_g[:, None])[..., None]
    k_dec = jnp.exp((c - 1.0 - idx[None, :]) * log_g[:, None])[..., None]
    c_dec = jnp.exp(c * log_g)[:, None, None]
    return d_intra, q_dec, k_dec, c_dec


def _head_norm_gate(o, g):
    mu = jnp.mean(o, -1, keepdims=True)
    d = o - mu
    var = jnp.mean(d * d, -1, keepdims=True)
    return _silu(g) * (d * lax.rsqrt(var + LN_EPS))


def _ret_step(q, k, v, s, d_intra, q_dec, k_dec, c_dec):
    vb = v.astype(BF16)
    sc = _bdot_nt(q, k) * d_intra
    o = jnp.dot(sc.astype(BF16), vb, preferred_element_type=F32) + _bdot(q * q_dec, s)
    s_new = s * c_dec + jnp.dot((k * k_dec).T.astype(BF16), vb, preferred_element_type=F32)
    return o, s_new


def _ret_prompt_kernel(q_ref, k_ref, v_ref, g_ref, di_ref, qd_ref, kd_ref, cd_ref, o_ref, so_ref, s_scr):
    ci = pl.program_id(2)

    @pl.when(ci == 0)
    def _():
        s_scr[...] = jnp.zeros(s_scr.shape, F32)

    o, s_new = _ret_step(q_ref[...], k_ref[...], v_ref[...], s_scr[...],
                         di_ref[...], qd_ref[...], kd_ref[...], cd_ref[...])
    s_scr[...] = s_new
    o_ref[...] = _head_norm_gate(o, g_ref[...])

    @pl.when(ci == pl.num_programs(2) - 1)
    def _():
        so_ref[...] = s_new


def _ret_prompt(qkvg, bsz, seq, chunk):
    nc = seq // chunk
    di, qd, kd, cd = _ret_tables(chunk)
    kcol = H_C
    vcol = 2 * H_C * DK_C // DV_C
    gcol = vcol + H_C
    return pl.pallas_call(
        _ret_prompt_kernel,
        out_shape=(jax.ShapeDtypeStruct((bsz * seq, H_C * DV_C), F32),
                   jax.ShapeDtypeStruct((bsz, H_C, DK_C, DV_C), F32)),
        grid=(bsz, H_C, nc),
        in_specs=[pl.BlockSpec((chunk, DK_C), lambda b, h, c: (b * nc + c, h)),
                  pl.BlockSpec((chunk, DK_C), lambda b, h, c: (b * nc + c, kcol + h)),
                  pl.BlockSpec((chunk, DV_C), lambda b, h, c: (b * nc + c, vcol + h)),
                  pl.BlockSpec((chunk, DV_C), lambda b, h, c: (b * nc + c, gcol + h)),
                  pl.BlockSpec((None, chunk, chunk), lambda b, h, c: (h, 0, 0)),
                  pl.BlockSpec((None, chunk, 1), lambda b, h, c: (h, 0, 0)),
                  pl.BlockSpec((None, chunk, 1), lambda b, h, c: (h, 0, 0)),
                  pl.BlockSpec((None, 1, 1), lambda b, h, c: (h, 0, 0))],
        out_specs=(pl.BlockSpec((chunk, DV_C), lambda b, h, c: (b * nc + c, h)),
                   pl.BlockSpec((None, None, DK_C, DV_C), lambda b, h, c: (b, h, 0, 0))),
        scratch_shapes=[pltpu.VMEM((DK_C, DV_C), F32)],
        compiler_params=_cparams(("parallel", "parallel", "arbitrary")),
        name="ret_prompt",
    )(qkvg, qkvg, qkvg, qkvg, di, qd, kd, cd)


def _ret_sample_kernel(q_ref, k_ref, v_ref, g_ref, s_ref, di_ref, qd_ref, kd_ref, cd_ref, o_ref, so_ref):
    for h in range(H_C):
        o, s_new = _ret_step(q_ref[:, h * DK_C:(h + 1) * DK_C], k_ref[:, h * DK_C:(h + 1) * DK_C],
                             v_ref[:, h * DV_C:(h + 1) * DV_C], s_ref[h],
                             di_ref[h], qd_ref[h], kd_ref[h], cd_ref[h])
        so_ref[h] = s_new
        o_ref[:, h * DV_C:(h + 1) * DV_C] = _head_norm_gate(o, g_ref[:, h * DV_C:(h + 1) * DV_C])


def _ret_sample(qkvg, state, ic, row0, dec_b, dec_seq):
    di, qd, kd, cd = _ret_tables(dec_seq)
    blk0 = row0 // dec_seq
    qw = H_C * DK_C
    vw = H_C * DV_C
    whole3 = lambda s: (0, 0, 0)
    return pl.pallas_call(
        _ret_sample_kernel,
        out_shape=(jax.ShapeDtypeStruct((dec_b * dec_seq, vw), F32),
                   jax.ShapeDtypeStruct((dec_b, H_C, DK_C, DV_C), F32)),
        grid=(dec_b,),
        in_specs=[pl.BlockSpec((dec_seq, qw), lambda s: (blk0 + s, 0)),
                  pl.BlockSpec((dec_seq, qw), lambda s: (blk0 + s, 1)),
                  pl.BlockSpec((dec_seq, vw), lambda s: (blk0 + s, 1)),
                  pl.BlockSpec((dec_seq, vw), lambda s: (blk0 + s, 2)),
                  pl.BlockSpec((None, None, H_C, DK_C, DV_C), lambda s: (ic, s, 0, 0, 0)),
                  pl.BlockSpec((H_C, dec_seq, dec_seq), whole3),
                  pl.BlockSpec((H_C, dec_seq, 1), whole3),
                  pl.BlockSpec((H_C, dec_seq, 1), whole3),
                  pl.BlockSpec((H_C, 1, 1), whole3)],
        out_specs=(pl.BlockSpec((dec_seq, vw), lambda s: (s, 0)),
                   pl.BlockSpec((None, H_C, DK_C, DV_C), lambda s: (s, 0, 0, 0))),
        compiler_params=_cparams(("arbitrary",)),
        name="ret_sample",
    )(qkvg, qkvg, qkvg, qkvg, state, di, qd, kd, cd)


def _route(logits):
    rows = [logits[e:e + 1, :] for e in range(N_EXPERTS)]
    mx = functools.reduce(jnp.maximum, rows)
    ex = [jnp.exp(r - mx) for r in rows]
    den = functools.reduce(lambda a, b: a + b, ex)
    probs = [e / den for e in ex]
    return probs


def _pick_top2(probs, bias_ref):
    sel = [probs[e] + bias_ref[e:e + 1, :] for e in range(N_EXPERTS)]
    epg = EXPERTS_PER_GROUP
    neg = jnp.full_like(sel[0], -jnp.inf)
    gscore = []
    for g in range(N_GROUPS):
        grp = sel[g * epg:(g + 1) * epg]
        pairs = [grp[a] + grp[b] for a in range(epg) for b in range(a + 1, epg)]
        gscore.append(functools.reduce(jnp.maximum, pairs))
    best = functools.reduce(jnp.maximum, gscore)
    g_idx = jnp.full(best.shape, N_GROUPS - 1, jnp.int32)
    for g in range(N_GROUPS - 2, -1, -1):
        g_idx = jnp.where(gscore[g] == best, g, g_idx)
    ing = []
    inp = []
    for j in range(epg):
        sv = sel[j]
        pv = probs[j]
        for g in range(1, N_GROUPS):
            sv = jnp.where(g_idx == g, sel[g * epg + j], sv)
            pv = jnp.where(g_idx == g, probs[g * epg + j], pv)
        ing.append(sv)
        inp.append(pv)
    top1 = functools.reduce(jnp.maximum, ing)
    l1 = jnp.full(best.shape, epg - 1, jnp.int32)
    for j in range(epg - 2, -1, -1):
        l1 = jnp.where(ing[j] == top1, j, l1)
    rest = [jnp.where(l1 == j, neg, ing[j]) for j in range(epg)]
    top2 = functools.reduce(jnp.maximum, rest)
    l2 = jnp.full(best.shape, epg - 1, jnp.int32)
    for j in range(epg - 2, -1, -1):
        l2 = jnp.where(jnp.logical_and(rest[j] == top2, l1 != j), j, l2)
    p1 = functools.reduce(lambda a, b: a + b, [jnp.where(l1 == j, inp[j], 0.0) for j in range(epg)])
    p2 = functools.reduce(lambda a, b: a + b, [jnp.where(l2 == j, inp[j], 0.0) for j in range(epg)])
    tot = p1 + p2
    return g_idx * epg + l1, g_idx * epg + l2, p1 / tot, p2 / tot


def _outproj_kernel(*refs, n_in):
    in_refs = refs[:n_in]
    x_ref, w_ref, lg_ref, lb_ref, wr_ref, br_ref, o_ref, gate_ref = refs[n_in:]
    a = in_refs[0][...] if n_in == 1 else jnp.concatenate([r[...] for r in in_refs], axis=1)
    mix = _bdot(a, w_ref[...])
    x1 = _layer_norm_rows(DN_ALPHA * x_ref[...] + mix, lg_ref[...], lb_ref[...])
    o_ref[...] = x1
    logits = lax.dot_general(wr_ref[...], x1, (((1,), (1,)), ((), ())),
                             precision=lax.Precision.HIGHEST, preferred_element_type=F32)
    probs = _route(logits)
    e1, e2, w1, w2 = _pick_top2(probs, br_ref)
    for e in range(N_EXPERTS):
        gate_ref[e:e + 1, :] = jnp.where(e1 == e, w1, 0.0) + jnp.where(e2 == e, w2, 0.0)


def _outproj(parts, x, w, li, lg, lb, w_router_t, b_router, tm, name):
    t_all = x.shape[0]
    kern = functools.partial(_outproj_kernel, n_in=len(parts))
    vec = lambda i: (0, 0)
    return pl.pallas_call(
        kern,
        out_shape=(jax.ShapeDtypeStruct((t_all, D_MODEL), F32),
                   jax.ShapeDtypeStruct((N_EXPERTS, t_all), F32)),
        grid=(t_all // tm,),
        in_specs=[pl.BlockSpec((tm, p.shape[1]), lambda i: (i, 0)) for p in parts]
                 + [pl.BlockSpec((tm, D_MODEL), lambda i: (i, 0)),
                    pl.BlockSpec((None,) + w.shape[1:], lambda i: (li, 0, 0)),
                    pl.BlockSpec((1, D_MODEL), vec),
                    pl.BlockSpec((1, D_MODEL), vec),
                    pl.BlockSpec((N_EXPERTS, D_MODEL), vec),
                    pl.BlockSpec((N_EXPERTS, 1), vec)],
        out_specs=(pl.BlockSpec((tm, D_MODEL), lambda i: (i, 0)),
                   pl.BlockSpec((N_EXPERTS, tm), lambda i: (0, i))),
        compiler_params=_cparams(("parallel",)),
        name=name,
    )(*parts, x, w, lg.reshape(1, -1), lb.reshape(1, -1), w_router_t, b_router.reshape(-1, 1))


def _moe_dense_kernel(x_ref, gate_ref, wg_ref, wu_ref, wd_ref, lg_ref, lb_ref, o_ref, acc_scr):
    e = pl.program_id(1)

    @pl.when(e == 0)
    def _():
        acc_scr[...] = jnp.zeros(acc_scr.shape, F32)

    xb = x_ref[...].astype(BF16)
    hid = _silu(jnp.dot(xb, wg_ref[...].astype(BF16), preferred_element_type=F32)) * \
        jnp.dot(xb, wu_ref[...].astype(BF16), preferred_element_type=F32)
    y = _bdot(hid, wd_ref[...])
    gates = gate_ref[...]
    lane = lax.broadcasted_iota(jnp.int32, gates.shape, 1)
    gcol = jnp.sum(jnp.where(lane == e, gates, 0.0), axis=1, keepdims=True)
    acc_scr[...] += gcol * y

    @pl.when(e == pl.num_programs(1) - 1)
    def _():
        o_ref[...] = _layer_norm_rows(DN_ALPHA * x_ref[...] + acc_scr[...], lg_ref[...], lb_ref[...])


def _moe_dense(x, gates_t, wg, wu, wd, layer, lg, lb, tm):
    t_all = x.shape[0]
    gates = gates_t.T
    vec = lambda i, e: (0, 0)
    return pl.pallas_call(
        _moe_dense_kernel,
        out_shape=jax.ShapeDtypeStruct((t_all, D_MODEL), F32),
        grid=(t_all // tm, N_EXPERTS),
        in_specs=[pl.BlockSpec((tm, D_MODEL), lambda i, e: (i, 0)),
                  pl.BlockSpec((tm, N_EXPERTS), lambda i, e: (i, 0)),
                  pl.BlockSpec((None, None, D_MODEL, D_FF), lambda i, e: (layer, e, 0, 0)),
                  pl.BlockSpec((None, None, D_MODEL, D_FF), lambda i, e: (layer, e, 0, 0)),
                  pl.BlockSpec((None, None, D_FF, D_MODEL), lambda i, e: (layer, e, 0, 0)),
                  pl.BlockSpec((1, D_MODEL), vec),
                  pl.BlockSpec((1, D_MODEL), vec)],
        out_specs=pl.BlockSpec((tm, D_MODEL), lambda i, e: (i, 0)),
        scratch_shapes=[pltpu.VMEM((tm, D_MODEL), F32)],
        compiler_params=_cparams(("parallel", "arbitrary")),
        name="moe_dense",
    )(x, gates, wg, wu, wd, lg.reshape(1, -1), lb.reshape(1, -1))


def kernel(x_prompt, x_sample, cache_k, cache_v, page_table, state_conv, state_ret, w_in_a, lambda_a, subln_a,
           conv_w, conv_b, conv_ln_g, conv_ln_b, w_out_a, w_in_c, w_out_c, ln_g, ln_b, w_router, b_router,
           w_e_gate, w_e_up, w_e_down):
    bsz, seq, _ = x_prompt.shape
    dec_b, dec_seq, _ = x_sample.shape
    n_pages = page_table.shape[1]
    past_len = n_pages * PAGE_SIZE
    tp = bsz * seq
    ts_ = dec_b * dec_seq
    tm = min(512, seq)
    assert seq % tm == 0 and ts_ % tm == 0 and dec_seq == SUBLANES
    seq_tiles = seq // tm
    n_prompt_tiles = tp // tm

    pos_p = jnp.arange(seq, dtype=jnp.int32)
    pos_s = past_len + (jnp.arange(tm, dtype=jnp.int32) % dec_seq)
    pos_tab = jnp.concatenate([pos_p, pos_s])
    tabs_a = _rot_tables_a(pos_tab)
    tabs_c = _rot_tables_c(pos_tab)

    n_phys = cache_k.shape[1]
    ck = cache_k.reshape(cache_k.shape[0], n_phys, PAGE_SIZE, QK_W)
    cv = cache_v.reshape(cache_v.shape[0], n_phys, PAGE_SIZE, ATT_W)
    w_router_t = w_router.T

    x = jnp.concatenate([x_prompt.reshape(tp, D_MODEL), x_sample.reshape(ts_, D_MODEL)], axis=0)
    k_p, v_p, conv_p, ret_p, k_s, v_s, conv_s, ret_s = [], [], [], [], [], [], [], []
    for l in range(DEPTH):
        if l % 2 == 0:
            ia = l // 2
            lam_init = 0.8 - 0.6 * math.exp(-0.3 * l)
            qkvag = _inproj(x, w_in_a, ia, tabs_a, _inproj_a_kernel, tm, seq_tiles, n_prompt_tiles, "inproj_a")
            att_p = _attn_prompt(qkvag, lambda_a[ia], subln_a[ia], bsz, seq, lam_init, min(256, seq))
            att_s = _attn_sample(qkvag, ck, cv, ia, page_table, lambda_a[ia], subln_a[ia], tp, dec_b,
                                 dec_seq, lam_init)
            c_p, st_p = _conv_prompt(qkvag, conv_w[ia], conv_b[ia], conv_ln_g[ia], conv_ln_b[ia], bsz, seq,
                                     min(256, seq))
            c_s, st_s = _conv_sample(qkvag, state_conv, ia, conv_w[ia], conv_b[ia], conv_ln_g[ia],
                                     conv_ln_b[ia], tp, dec_b, dec_seq, 8)
            att = jnp.concatenate([att_p, att_s], axis=0)
            cc = jnp.concatenate([c_p, c_s], axis=0)
            x, gates_t = _outproj([att, cc], x, w_out_a, ia, ln_g[l, 0], ln_b[l, 0], w_router_t, b_router, tm,
                                  "outproj_a")
            k_all = qkvag[:, QK_W:2 * QK_W]
            v_all = qkvag[:, 2 * QK_W:2 * QK_W + ATT_W]
            k_p.append(k_all[:tp].reshape(bsz, seq, 2 * H_A, D_QK))
            v_p.append(v_all[:tp].reshape(bsz, seq, H_A, D_VA))
            k_s.append(k_all[tp:].reshape(dec_b, dec_seq, 2 * H_A, D_QK))
            v_s.append(v_all[tp:].reshape(dec_b, dec_seq, H_A, D_VA))
            conv_p.append(st_p)
            conv_s.append(st_s)
        else:
            ic = l // 2
            qkvg = _inproj(x, w_in_c, ic, tabs_c, _inproj_c_kernel, tm, seq_tiles, n_prompt_tiles, "inproj_c")
            og_p, s_p = _ret_prompt(qkvg, bsz, seq, min(256, seq))
            og_s, s_s = _ret_sample(qkvg, state_ret, ic, tp, dec_b, dec_seq)
            og = jnp.concatenate([og_p, og_s], axis=0)
            x, gates_t = _outproj([og], x, w_out_c, ic, ln_g[l, 0], ln_b[l, 0], w_router_t, b_router, tm,
                                  "outproj_c")
            ret_p.append(s_p)
            ret_s.append(s_s)
        x = _moe_dense(x, gates_t, w_e_gate, w_e_up, w_e_down, l, ln_g[l, 1], ln_b[l, 1], tm)

    y_prompt = x[:tp].reshape(bsz, seq, D_MODEL)
    y_sample = x[tp:].reshape(dec_b, dec_seq, D_MODEL)
    return (y_prompt, y_sample, jnp.stack(k_p), jnp.stack(v_p), jnp.stack(conv_p), jnp.stack(ret_p),
            jnp.stack(k_s), jnp.stack(v_s), jnp.stack(conv_s), jnp.stack(ret_s))
```

```python
import functools
import math

import jax
import jax.numpy as jnp
from jax import lax
from jax.experimental import pallas as pl
from jax.experimental.pallas import tpu as pltpu

F32 = jnp.float32
BF16 = jnp.bfloat16

D_MODEL = 1024
DEPTH = 4
PAGE_SIZE = 128
H_A = 4
D_QK = 64
D_VA = 2 * D_QK
QK_W = H_A * 2 * D_QK
ATT_W = H_A * D_VA
ROT_DIM = D_QK // 4
ROPE_THETA = 500000.0
CONV_CH = D_MODEL // 2
CONV_W = 31
H_C = 4
DK_C = D_MODEL // H_C
DV_C = 2 * DK_C
RET_THETA = 10000.0
N_EXPERTS = 16
N_GROUPS = 4
EXPERTS_PER_GROUP = N_EXPERTS // N_GROUPS
D_FF = D_MODEL // 2
DN_ALPHA = (2 * DEPTH) ** 0.25
LN_EPS = 1e-5
IN_A = 2 * QK_W + ATT_W + 2 * CONV_CH
IN_C = 2 * H_C * DK_C + 2 * H_C * DV_C

LANES = 128
SUBLANES = 8
CONV_HALO = 32
VMEM_LIMIT = 56 * 1024 * 1024
MOE_TILE = 256
MOE_CHUNK = 1024


def _cparams(sem):
    return pltpu.CompilerParams(dimension_semantics=sem, vmem_limit_bytes=VMEM_LIMIT)


def _bdot(a, b):
    return jnp.dot(a.astype(BF16), b.astype(BF16), preferred_element_type=F32)


def _bdot_nt(a, b):
    return lax.dot_general(a.astype(BF16), b.astype(BF16), (((1,), (1,)), ((), ())),
                           preferred_element_type=F32)


def _layer_norm_rows(v, g, b):
    mu = jnp.mean(v, -1, keepdims=True)
    d = v - mu
    var = jnp.mean(d * d, -1, keepdims=True)
    return d * lax.rsqrt(var + LN_EPS) * g + b


def _silu(v):
    return v * (1.0 / (1.0 + jnp.exp(-v)))


def _sigmoid(v):
    return 1.0 / (1.0 + jnp.exp(-v))


def _cast_rows_once(x_ref, xb_ref):
    @pl.when(pl.program_id(1) == 0)
    def _():
        xb_ref[...] = x_ref[...].astype(BF16)


def _inproj_a_kernel(x_ref, w_ref, cos_ref, sa_ref, sb_ref, o_ref, xb_ref):
    j = pl.program_id(1)
    _cast_rows_once(x_ref, xb_ref)
    y = jnp.dot(xb_ref[...], w_ref[...], preferred_element_type=F32)

    @pl.when(j < 2)
    def _():
        cos = cos_ref[...]
        sa = sa_ref[...]
        sb = sb_ref[...]
        for blk in range(QK_W // LANES):
            t = y[:, blk * LANES:(blk + 1) * LANES]
            r = (t * cos + pltpu.roll(t, ROT_DIM // 2, 1) * sa
                 + pltpu.roll(t, LANES - ROT_DIM // 2, 1) * sb)
            o_ref[:, blk * LANES:(blk + 1) * LANES] = r

    @pl.when(j >= 2)
    def _():
        o_ref[...] = y


def _rot_tables_a(pos):
    half = ROT_DIM // 2
    inv = ROPE_THETA ** (-jnp.arange(0, ROT_DIM, 2, dtype=F32) / ROT_DIM)
    ang = pos.astype(F32)[:, None] * inv[None, :]
    c, s = jnp.cos(ang), jnp.sin(ang)
    n = pos.shape[0]
    one = jnp.ones((n, D_QK - ROT_DIM), F32)
    zero = jnp.zeros((n, D_QK - ROT_DIM), F32)
    zh = jnp.zeros((n, half), F32)
    cos64 = jnp.concatenate([c, c, one], 1)
    sa64 = jnp.concatenate([zh, s, zero], 1)
    sb64 = jnp.concatenate([-s, zh, zero], 1)
    rep = LANES // D_QK
    return jnp.tile(cos64, (1, rep)), jnp.tile(sa64, (1, rep)), jnp.tile(sb64, (1, rep))


def _inproj_c_kernel(x_ref, w_ref, cos_ref, sin_ref, o_ref, xb_ref):
    j = pl.program_id(1)
    _cast_rows_once(x_ref, xb_ref)
    y = jnp.dot(xb_ref[...], w_ref[...], preferred_element_type=F32)

    @pl.when(j < 4)
    def _():
        cos = cos_ref[...]
        sin = sin_ref[...]
        scale = jnp.where(j >= 2, DK_C ** -0.5, 1.0).astype(F32)
        half = DK_C // 2
        for hd in range(2):
            t1 = y[:, hd * DK_C:hd * DK_C + half]
            t2 = y[:, hd * DK_C + half:(hd + 1) * DK_C]
            o_ref[:, hd * DK_C:hd * DK_C + half] = (t1 * cos - t2 * sin) * scale
            o_ref[:, hd * DK_C + half:(hd + 1) * DK_C] = (t2 * cos + t1 * sin) * scale

    @pl.when(j >= 4)
    def _():
        o_ref[...] = y


def _rot_tables_c(pos):
    inv = RET_THETA ** (-jnp.linspace(0.0, 1.0, DK_C // 2, dtype=F32))
    ang = pos.astype(F32)[:, None] * inv[None, :]
    return jnp.cos(ang), jnp.sin(ang)


def _inproj(x, w, li, tables, kern, tm, seq_tiles, n_prompt_tiles, name):
    t_all, d = x.shape
    n = w.shape[2]
    tn = 512
    tw = tables[0].shape[1]

    def tab_map(i, j):
        return (jnp.where(i < n_prompt_tiles, i % seq_tiles, seq_tiles), 0)

    return pl.pallas_call(
        kern,
        out_shape=jax.ShapeDtypeStruct((t_all, n), F32),
        grid=(t_all // tm, n // tn),
        in_specs=[pl.BlockSpec((tm, d), lambda i, j: (i, 0)),
                  pl.BlockSpec((None, d, tn), lambda i, j: (li, 0, j))]
                 + [pl.BlockSpec((tm, tw), tab_map) for _ in tables],
        out_specs=pl.BlockSpec((tm, tn), lambda i, j: (i, j)),
        scratch_shapes=[pltpu.VMEM((tm, d), BF16)],
        compiler_params=_cparams(("parallel", "arbitrary")),
        name=name,
    )(x, w, *tables)


def _lambda_value(lam_ref, lam_init):
    lf = lam_ref[...]
    a = jnp.sum(lf[0:1, :] * lf[1:2, :], axis=-1, keepdims=True)
    b = jnp.sum(lf[2:3, :] * lf[3:4, :], axis=-1, keepdims=True)
    return jnp.exp(a) - jnp.exp(b) + lam_init


def _sub_norm(o, g, lam_init):
    ms = jnp.mean(o * o, -1, keepdims=True)
    return o * lax.rsqrt(ms + LN_EPS) * g * (1.0 - lam_init)


def _attn_prompt_kernel(q_ref, k_ref, v_ref, lam_ref, g_ref, o_ref, *, lam_init, tq):
    qi = pl.program_id(2)
    q = q_ref[...] * (D_QK ** -0.5)
    lane = lax.broadcasted_iota(jnp.int32, q.shape, 1)
    row = lax.broadcasted_iota(jnp.int32, (tq, tq), 0)
    col = lax.broadcasted_iota(jnp.int32, (tq, tq), 1)
    outs = []
    for m in range(2):
        qm = jnp.where(lane // D_QK == m, q, 0.0).astype(BF16)

        def block(j, carry, diagonal, qm=qm):
            m_prev, l_prev, acc = carry
            start = pl.multiple_of(j * tq, tq)
            s = _bdot_nt(qm, k_ref[pl.ds(start, tq), :])
            if diagonal:
                s = jnp.where(col <= row, s, -jnp.inf)
            m_new = jnp.maximum(m_prev, jnp.max(s, -1, keepdims=True))
            alpha = jnp.exp(m_prev - m_new)
            p = jnp.exp(s - m_new)
            l_new = alpha * l_prev + jnp.sum(p, -1, keepdims=True)
            acc = alpha * acc + _bdot(p, v_ref[pl.ds(start, tq), :])
            return m_new, l_new, acc

        init = (jnp.full((tq, 1), -jnp.inf, F32), jnp.zeros((tq, 1), F32), jnp.zeros((tq, D_VA), F32))
        carry = lax.fori_loop(0, qi, functools.partial(block, diagonal=False), init)
        _, l_fin, acc = block(qi, carry, True)
        outs.append(acc / l_fin)
    lam = _lambda_value(lam_ref, lam_init)
    o_ref[...] = _sub_norm(outs[0] - lam * outs[1], g_ref[...], lam_init)


def _attn_prompt(qkvag, lam_p, subln_g, bsz, seq, lam_init, tq):
    nq = seq // tq
    kern = functools.partial(_attn_prompt_kernel, lam_init=lam_init, tq=tq)
    kcol = QK_W // D_VA
    vcol = 2 * QK_W // D_VA
    return pl.pallas_call(
        kern,
        out_shape=jax.ShapeDtypeStruct((bsz * seq, ATT_W), F32),
        grid=(bsz, H_A, nq),
        in_specs=[pl.BlockSpec((tq, D_VA), lambda b, h, qi: (b * nq + qi, h)),
                  pl.BlockSpec((seq, D_VA), lambda b, h, qi: (b, kcol + h)),
                  pl.BlockSpec((seq, D_VA), lambda b, h, qi: (b, vcol + h)),
                  pl.BlockSpec((4, D_QK), lambda b, h, qi: (0, 0)),
                  pl.BlockSpec((1, D_VA), lambda b, h, qi: (0, 0))],
        out_specs=pl.BlockSpec((tq, D_VA), lambda b, h, qi: (b * nq + qi, h)),
        compiler_params=_cparams(("parallel", "parallel", "arbitrary")),
        name="attn_prompt",
    )(qkvag, qkvag, qkvag, lam_p, subln_g.reshape(1, D_VA))


def _attn_sample_kernel(pt_ref, q_ref, kn_ref, vn_ref, lam_ref, g_ref, *rest, lam_init, n_pages, dec_seq):
    k_refs = rest[:n_pages]
    v_refs = rest[n_pages:2 * n_pages]
    o_ref = rest[2 * n_pages]
    nrow = 2 * H_A * dec_seq
    q = q_ref[...] * (D_QK ** -0.5)
    qt = jnp.concatenate([q] * (2 * H_A), axis=0)
    rid = lax.broadcasted_iota(jnp.int32, (nrow, QK_W), 0)
    cid = lax.broadcasted_iota(jnp.int32, (nrow, QK_W), 1)
    qbd = jnp.where(cid // D_QK == rid // dec_seq, qt, 0.0).astype(BF16)

    s_past = [_bdot(qbd, kr[...].reshape(QK_W, PAGE_SIZE)) for kr in k_refs]
    s_new = _bdot_nt(qbd, kn_ref[...])
    qpos = lax.broadcasted_iota(jnp.int32, (nrow, dec_seq), 0) % dec_seq
    kpos = lax.broadcasted_iota(jnp.int32, (nrow, dec_seq), 1)
    s_new = jnp.where(kpos <= qpos, s_new, -jnp.inf)
    m = jnp.max(s_new, -1, keepdims=True)
    for s in s_past:
        m = jnp.maximum(m, jnp.max(s, -1, keepdims=True))
    p_new = jnp.exp(s_new - m)
    l = jnp.sum(p_new, -1, keepdims=True)
    p_past = []
    for s in s_past:
        p = jnp.exp(s - m)
        l = l + jnp.sum(p, -1, keepdims=True)
        p_past.append(p.astype(BF16))
    lam = _lambda_value(lam_ref, lam_init)
    g = g_ref[...]
    grp = 2 * dec_seq
    for h in range(H_A):
        r0 = h * grp
        acc = _bdot(p_new[r0:r0 + grp], vn_ref[:, h * D_VA:(h + 1) * D_VA])
        for p, vr in zip(p_past, v_refs):
            acc = acc + _bdot(p[r0:r0 + grp], vr[pl.ds(h, PAGE_SIZE, stride=H_A), :])
        acc = acc / l[r0:r0 + grp]
        o = acc[:dec_seq] - lam * acc[dec_seq:]
        o_ref[:, h * D_VA:(h + 1) * D_VA] = _sub_norm(o, g, lam_init)


def _attn_sample(qkvag, cache_kt, cache_vr, ia, page_table, lam_p, subln_g, row0, dec_b, dec_seq, lam_init):
    n_pages = page_table.shape[1]
    kern = functools.partial(_attn_sample_kernel, lam_init=lam_init, n_pages=n_pages, dec_seq=dec_seq)
    blk0 = row0 // dec_seq

    def k_spec(p):
        return pl.BlockSpec((None, None, 2 * H_A, D_QK, PAGE_SIZE), lambda s, pt: (ia, pt[s, p], 0, 0, 0))

    def v_spec(p):
        return pl.BlockSpec((None, None, PAGE_SIZE * H_A, D_VA), lambda s, pt: (ia, pt[s, p], 0, 0))

    grid_spec = pltpu.PrefetchScalarGridSpec(
        num_scalar_prefetch=1,
        grid=(dec_b,),
        in_specs=[pl.BlockSpec((dec_seq, QK_W), lambda s, pt: (blk0 + s, 0)),
                  pl.BlockSpec((dec_seq, QK_W), lambda s, pt: (blk0 + s, 1)),
                  pl.BlockSpec((dec_seq, ATT_W), lambda s, pt: (blk0 + s, 2)),
                  pl.BlockSpec((4, D_QK), lambda s, pt: (0, 0)),
                  pl.BlockSpec((1, D_VA), lambda s, pt: (0, 0))]
                 + [k_spec(p) for p in range(n_pages)]
                 + [v_spec(p) for p in range(n_pages)],
        out_specs=pl.BlockSpec((dec_seq, ATT_W), lambda s, pt: (s, 0)),
    )
    return pl.pallas_call(
        kern,
        out_shape=jax.ShapeDtypeStruct((dec_b * dec_seq, ATT_W), F32),
        grid_spec=grid_spec,
        compiler_params=_cparams(("arbitrary",)),
        name="attn_sample",
    )(page_table, qkvag, qkvag, qkvag, lam_p, subln_g.reshape(1, D_VA),
      *([cache_kt] * n_pages), *([cache_vr] * n_pages))


def _conv_taps(scr, cw_ref, row0, rows):
    acc = None
    for w in range(CONV_W):
        term = scr[pl.ds(row0 + w, rows), :] * cw_ref[w:w + 1, :]
        acc = term if acc is None else acc + term
    return acc


def _conv_prompt_kernel(a_ref, g_ref, ah_ref, gh_ref, cw_ref, cb_ref, lg_ref, lb_ref,
                        c_ref, st_ref, scr, *, ts, chunk):
    i = pl.program_id(1)
    n = pl.num_programs(1)
    u = a_ref[...] * _sigmoid(g_ref[...])
    uh = ah_ref[...] * _sigmoid(gh_ref[...])
    uh = jnp.where(i > 0, uh, 0.0)
    scr[0:CONV_HALO, :] = uh
    scr[CONV_HALO:CONV_HALO + ts, :] = u
    off = CONV_HALO - (CONV_W - 1)
    for c0 in range(0, ts, chunk):
        c = _conv_taps(scr, cw_ref, c0 + off, chunk) + cb_ref[...]
        c_ref[c0:c0 + chunk, :] = _silu(_layer_norm_rows(c, lg_ref[...], lb_ref[...]))

    @pl.when(i == n - 1)
    def _():
        st_ref[...] = scr[CONV_HALO + ts - (CONV_W - 1):CONV_HALO + ts, :]


def _conv_prompt(qkvag, cw, cb, lg, lb, bsz, seq, ts):
    ns = seq // ts
    hb = ts // CONV_HALO
    acol = (2 * QK_W + ATT_W) // CONV_CH
    kern = functools.partial(_conv_prompt_kernel, ts=ts, chunk=32)
    vec = lambda b, i: (0, 0)
    return pl.pallas_call(
        kern,
        out_shape=(jax.ShapeDtypeStruct((bsz * seq, CONV_CH), F32),
                   jax.ShapeDtypeStruct((bsz, CONV_W - 1, CONV_CH), F32)),
        grid=(bsz, ns),
        in_specs=[pl.BlockSpec((ts, CONV_CH), lambda b, i: (b * ns + i, acol)),
                  pl.BlockSpec((ts, CONV_CH), lambda b, i: (b * ns + i, acol + 1)),
                  pl.BlockSpec((CONV_HALO, CONV_CH), lambda b, i: (jnp.maximum((b * ns + i) * hb - 1, 0), acol)),
                  pl.BlockSpec((CONV_HALO, CONV_CH), lambda b, i: (jnp.maximum((b * ns + i) * hb - 1, 0), acol + 1)),
                  pl.BlockSpec((CONV_W, CONV_CH), vec),
                  pl.BlockSpec((1, CONV_CH), vec),
                  pl.BlockSpec((1, CONV_CH), vec),
                  pl.BlockSpec((1, CONV_CH), vec)],
        out_specs=(pl.BlockSpec((ts, CONV_CH), lambda b, i: (b * ns + i, 0)),
                   pl.BlockSpec((None, CONV_W - 1, CONV_CH), lambda b, i: (b, 0, 0))),
        scratch_shapes=[pltpu.VMEM((CONV_HALO + ts, CONV_CH), F32)],
        compiler_params=_cparams(("parallel", "arbitrary")),
        name="conv_prompt",
    )(qkvag, qkvag, qkvag, qkvag, cw, cb.reshape(1, -1), lg.reshape(1, -1), lb.reshape(1, -1))


def _conv_sample_kernel(a_ref, g_ref, st_ref, cw_ref, cb_ref, lg_ref, lb_ref, c_ref, so_ref, scr, *, nb, dec_seq):
    hist = CONV_W - 1
    u = a_ref[...] * _sigmoid(g_ref[...])
    for s in range(nb):
        scr[0:hist, :] = st_ref[s]
        scr[hist:hist + dec_seq, :] = u[s * dec_seq:(s + 1) * dec_seq, :]
        c = _conv_taps(scr, cw_ref, 0, dec_seq) + cb_ref[...]
        c_ref[s * dec_seq:(s + 1) * dec_seq, :] = _silu(_layer_norm_rows(c, lg_ref[...], lb_ref[...]))
        so_ref[s] = scr[dec_seq:dec_seq + hist, :]


def _conv_sample(qkvag, state, ia, cw, cb, lg, lb, row0, dec_b, dec_seq, nb):
    acol = (2 * QK_W + ATT_W) // CONV_CH
    blk0 = row0 // (nb * dec_seq)
    kern = functools.partial(_conv_sample_kernel, nb=nb, dec_seq=dec_seq)
    vec = lambda i: (0, 0)
    return pl.pallas_call(
        kern,
        out_shape=(jax.ShapeDtypeStruct((dec_b * dec_seq, CONV_CH), F32),
                   jax.ShapeDtypeStruct((dec_b, CONV_W - 1, CONV_CH), F32)),
        grid=(dec_b // nb,),
        in_specs=[pl.BlockSpec((nb * dec_seq, CONV_CH), lambda i: (blk0 + i, acol)),
                  pl.BlockSpec((nb * dec_seq, CONV_CH), lambda i: (blk0 + i, acol + 1)),
                  pl.BlockSpec((None, nb, CONV_W - 1, CONV_CH), lambda i: (ia, i, 0, 0)),
                  pl.BlockSpec((CONV_W, CONV_CH), vec),
                  pl.BlockSpec((1, CONV_CH), vec),
                  pl.BlockSpec((1, CONV_CH), vec),
                  pl.BlockSpec((1, CONV_CH), vec)],
        out_specs=(pl.BlockSpec((nb * dec_seq, CONV_CH), lambda i: (i, 0)),
                   pl.BlockSpec((nb, CONV_W - 1, CONV_CH), lambda i: (i, 0, 0))),
        scratch_shapes=[pltpu.VMEM((CONV_W - 1 + dec_seq + SUBLANES, CONV_CH), F32)],
        compiler_params=_cparams(("arbitrary",)),
        name="conv_sample",
    )(qkvag, qkvag, state, cw, cb.reshape(1, -1), lg.reshape(1, -1), lb.reshape(1, -1))


def _ret_tables(c):
    log_g = jnp.log1p(-jnp.exp2(-5.0 - jnp.arange(H_C, dtype=F32)))
    idx = jnp.arange(c, dtype=F32)
    diff = idx[:, None] - idx[None, :]
    causal = diff >= 0
    d_intra = jnp.where(causal[None], jnp.exp(jnp.where(causal, diff, 0.0)[None] * log_g[:, None, None]), 0.0)
    q_dec = jnp.exp((idx[None, :] + 1.0) * log_g[:, None])[..., None]
    k_dec = jnp.exp((c - 1.0 - idx[None, :]) * log_g[:, None])[..., None]
    c_dec = jnp.exp(c * log_g)[:, None, None]
    return d_intra, q_dec, k_dec, c_dec


def _head_norm_gate(o, g):
    mu = jnp.mean(o, -1, keepdims=True)
    d = o - mu
    var = jnp.mean(d * d, -1, keepdims=True)
    return _silu(g) * (d * lax.rsqrt(var + LN_EPS))


def _ret_step(q, k, v, s, d_intra, q_dec, k_dec, c_dec):
    vb = v.astype(BF16)
    sc = _bdot_nt(q, k) * d_intra
    o = jnp.dot(sc.astype(BF16), vb, preferred_element_type=F32) + _bdot(q * q_dec, s)
    s_new = s * c_dec + jnp.dot((k * k_dec).T.astype(BF16), vb, preferred_element_type=F32)
    return o, s_new


def _ret_prompt_kernel(q_ref, k_ref, v_ref, g_ref, di_ref, qd_ref, kd_ref, cd_ref, o_ref, so_ref, s_scr):
    ci = pl.program_id(2)

    @pl.when(ci == 0)
    def _():
        s_scr[...] = jnp.zeros(s_scr.shape, F32)

    o, s_new = _ret_step(q_ref[...], k_ref[...], v_ref[...], s_scr[...],
                         di_ref[...], qd_ref[...], kd_ref[...], cd_ref[...])
    s_scr[...] = s_new
    o_ref[...] = _head_norm_gate(o, g_ref[...])

    @pl.when(ci == pl.num_programs(2) - 1)
    def _():
        so_ref[...] = s_new


def _ret_prompt(qkvg, bsz, seq, chunk):
    nc = seq // chunk
    di, qd, kd, cd = _ret_tables(chunk)
    kcol = H_C
    vcol = 2 * H_C * DK_C // DV_C
    gcol = vcol + H_C
    return pl.pallas_call(
        _ret_prompt_kernel,
        out_shape=(jax.ShapeDtypeStruct((bsz * seq, H_C * DV_C), F32),
                   jax.ShapeDtypeStruct((bsz, H_C, DK_C, DV_C), F32)),
        grid=(bsz, H_C, nc),
        in_specs=[pl.BlockSpec((chunk, DK_C), lambda b, h, c: (b * nc + c, h)),
                  pl.BlockSpec((chunk, DK_C), lambda b, h, c: (b * nc + c, kcol + h)),
                  pl.BlockSpec((chunk, DV_C), lambda b, h, c: (b * nc + c, vcol + h)),
                  pl.BlockSpec((chunk, DV_C), lambda b, h, c: (b * nc + c, gcol + h)),
                  pl.BlockSpec((None, chunk, chunk), lambda b, h, c: (h, 0, 0)),
                  pl.BlockSpec((None, chunk, 1), lambda b, h, c: (h, 0, 0)),
                  pl.BlockSpec((None, chunk, 1), lambda b, h, c: (h, 0, 0)),
                  pl.BlockSpec((None, 1, 1), lambda b, h, c: (h, 0, 0))],
        out_specs=(pl.BlockSpec((chunk, DV_C), lambda b, h, c: (b * nc + c, h)),
                   pl.BlockSpec((None, None, DK_C, DV_C), lambda b, h, c: (b, h, 0, 0))),
        scratch_shapes=[pltpu.VMEM((DK_C, DV_C), F32)],
        compiler_params=_cparams(("parallel", "parallel", "arbitrary")),
        name="ret_prompt",
    )(qkvg, qkvg, qkvg, qkvg, di, qd, kd, cd)


def _ret_sample_kernel(q_ref, k_ref, v_ref, g_ref, s_ref, di_ref, qd_ref, kd_ref, cd_ref, o_ref, so_ref):
    for h in range(H_C):
        o, s_new = _ret_step(q_ref[:, h * DK_C:(h + 1) * DK_C], k_ref[:, h * DK_C:(h + 1) * DK_C],
                             v_ref[:, h * DV_C:(h + 1) * DV_C], s_ref[h],
                             di_ref[h], qd_ref[h], kd_ref[h], cd_ref[h])
        so_ref[h] = s_new
        o_ref[:, h * DV_C:(h + 1) * DV_C] = _head_norm_gate(o, g_ref[:, h * DV_C:(h + 1) * DV_C])


def _ret_sample(qkvg, state, ic, row0, dec_b, dec_seq):
    di, qd, kd, cd = _ret_tables(dec_seq)
    blk0 = row0 // dec_seq
    qw = H_C * DK_C
    vw = H_C * DV_C
    whole3 = lambda s: (0, 0, 0)
    return pl.pallas_call(
        _ret_sample_kernel,
        out_shape=(jax.ShapeDtypeStruct((dec_b * dec_seq, vw), F32),
                   jax.ShapeDtypeStruct((dec_b, H_C, DK_C, DV_C), F32)),
        grid=(dec_b,),
        in_specs=[pl.BlockSpec((dec_seq, qw), lambda s: (blk0 + s, 0)),
                  pl.BlockSpec((dec_seq, qw), lambda s: (blk0 + s, 1)),
                  pl.BlockSpec((dec_seq, vw), lambda s: (blk0 + s, 1)),
                  pl.BlockSpec((dec_seq, vw), lambda s: (blk0 + s, 2)),
                  pl.BlockSpec((None, None, H_C, DK_C, DV_C), lambda s: (ic, s, 0, 0, 0)),
                  pl.BlockSpec((H_C, dec_seq, dec_seq), whole3),
                  pl.BlockSpec((H_C, dec_seq, 1), whole3),
                  pl.BlockSpec((H_C, dec_seq, 1), whole3),
                  pl.BlockSpec((H_C, 1, 1), whole3)],
        out_specs=(pl.BlockSpec((dec_seq, vw), lambda s: (s, 0)),
                   pl.BlockSpec((None, H_C, DK_C, DV_C), lambda s: (s, 0, 0, 0))),
        compiler_params=_cparams(("arbitrary",)),
        name="ret_sample",
    )(qkvg, qkvg, qkvg, qkvg, state, di, qd, kd, cd)


def _route(logits):
    rows = [logits[e:e + 1, :] for e in range(N_EXPERTS)]
    mx = functools.reduce(jnp.maximum, rows)
    ex = [jnp.exp(r - mx) for r in rows]
    den = functools.reduce(lambda a, b: a + b, ex)
    probs = [e / den for e in ex]
    return probs


def _pick_top2(probs, bias_ref):
    sel = [probs[e] + bias_ref[e:e + 1, :] for e in range(N_EXPERTS)]
    epg = EXPERTS_PER_GROUP
    neg = jnp.full_like(sel[0], -jnp.inf)
    gscore = []
    for g in range(N_GROUPS):
        grp = sel[g * epg:(g + 1) * epg]
        pairs = [grp[a] + grp[b] for a in range(epg) for b in range(a + 1, epg)]
        gscore.append(functools.reduce(jnp.maximum, pairs))
    best = functools.reduce(jnp.maximum, gscore)
    g_idx = jnp.full(best.shape, N_GROUPS - 1, jnp.int32)
    for g in range(N_GROUPS - 2, -1, -1):
        g_idx = jnp.where(gscore[g] == best, g, g_idx)
    ing = []
    inp = []
    for j in range(epg):
        sv = sel[j]
        pv = probs[j]
        for g in range(1, N_GROUPS):
            sv = jnp.where(g_idx == g, sel[g * epg + j], sv)
            pv = jnp.where(g_idx == g, probs[g * epg + j], pv)
        ing.append(sv)
        inp.append(pv)
    top1 = functools.reduce(jnp.maximum, ing)
    l1 = jnp.full(best.shape, epg - 1, jnp.int32)
    for j in range(epg - 2, -1, -1):
        l1 = jnp.where(ing[j] == top1, j, l1)
    rest = [jnp.where(l1 == j, neg, ing[j]) for j in range(epg)]
    top2 = functools.reduce(jnp.maximum, rest)
    l2 = jnp.full(best.shape, epg - 1, jnp.int32)
    for j in range(epg - 2, -1, -1):
        l2 = jnp.where(jnp.logical_and(rest[j] == top2, l1 != j), j, l2)
    p1 = functools.reduce(lambda a, b: a + b, [jnp.where(l1 == j, inp[j], 0.0) for j in range(epg)])
    p2 = functools.reduce(lambda a, b: a + b, [jnp.where(l2 == j, inp[j], 0.0) for j in range(epg)])
    tot = p1 + p2
    return g_idx * epg + l1, g_idx * epg + l2, p1 / tot, p2 / tot


def _outproj_kernel(*refs, n_in):
    in_refs = refs[:n_in]
    x_ref, w_ref, lg_ref, lb_ref, wr_ref, br_ref, o_ref, eidx_ref, ew_ref = refs[n_in:]
    a = in_refs[0][...] if n_in == 1 else jnp.concatenate([r[...] for r in in_refs], axis=1)
    mix = _bdot(a, w_ref[...])
    x1 = _layer_norm_rows(DN_ALPHA * x_ref[...] + mix, lg_ref[...], lb_ref[...])
    o_ref[...] = x1
    logits = lax.dot_general(wr_ref[...], x1, (((1,), (1,)), ((), ())),
                             precision=lax.Precision.HIGHEST, preferred_element_type=F32)
    probs = _route(logits)
    e1, e2, w1, w2 = _pick_top2(probs, br_ref)
    eidx_ref[0:1, :] = e1
    eidx_ref[1:2, :] = e2
    ew_ref[0:1, :] = w1
    ew_ref[1:2, :] = w2


def _outproj(parts, x, w, li, lg, lb, w_router_t, b_router, tm, name):
    t_all = x.shape[0]
    kern = functools.partial(_outproj_kernel, n_in=len(parts))
    vec = lambda i: (0, 0)
    return pl.pallas_call(
        kern,
        out_shape=(jax.ShapeDtypeStruct((t_all, D_MODEL), F32),
                   jax.ShapeDtypeStruct((2, t_all), jnp.int32),
                   jax.ShapeDtypeStruct((2, t_all), F32)),
        grid=(t_all // tm,),
        in_specs=[pl.BlockSpec((tm, p.shape[1]), lambda i: (i, 0)) for p in parts]
                 + [pl.BlockSpec((tm, D_MODEL), lambda i: (i, 0)),
                    pl.BlockSpec((None,) + w.shape[1:], lambda i: (li, 0, 0)),
                    pl.BlockSpec((1, D_MODEL), vec),
                    pl.BlockSpec((1, D_MODEL), vec),
                    pl.BlockSpec((N_EXPERTS, D_MODEL), vec),
                    pl.BlockSpec((N_EXPERTS, 1), vec)],
        out_specs=(pl.BlockSpec((tm, D_MODEL), lambda i: (i, 0)),
                   pl.BlockSpec((2, tm), lambda i: (0, i)),
                   pl.BlockSpec((2, tm), lambda i: (0, i))),
        compiler_params=_cparams(("parallel",)),
        name=name,
    )(*parts, x, w, lg.reshape(1, -1), lb.reshape(1, -1), w_router_t, b_router.reshape(-1, 1))


def _moe_plan(e_idx, tile):
    n_assign = e_idx.size
    e = e_idx.reshape(n_assign)
    onehot = (e[:, None] == jnp.arange(N_EXPERTS, dtype=jnp.int32)[None, :]).astype(jnp.int32)
    csum = jnp.cumsum(onehot, axis=0)
    counts = csum[-1]
    ends = jnp.cumsum(counts)
    starts = ends - counts
    pos = jnp.sum(onehot * (starts[None, :] + csum - 1), axis=1)
    n_tiles = n_assign // tile
    n_steps = n_tiles + N_EXPERTS - 1
    first_tile = starts // tile
    tiles_e = jnp.where(counts > 0, (ends + tile - 1) // tile - first_tile, 0)
    step_end = jnp.cumsum(tiles_e)
    step_start = step_end - tiles_e
    total = step_end[-1]
    sidx = jnp.arange(n_steps, dtype=jnp.int32)
    valid = sidx < total
    s_eff = jnp.minimum(sidx, total - 1)
    exp_s = jnp.minimum(jnp.searchsorted(step_end, s_eff, side="right"), N_EXPERTS - 1).astype(jnp.int32)
    tile_s = first_tile[exp_s] + (s_eff - step_start[exp_s])
    lo = jnp.maximum(starts[exp_s], tile_s * tile) - tile_s * tile
    hi = jnp.minimum(ends[exp_s], (tile_s + 1) * tile) - tile_s * tile
    prev_tile = jnp.concatenate([jnp.full((1,), -1, jnp.int32), tile_s[:-1]])
    first = jnp.logical_and(valid, tile_s != prev_tile)
    i32 = lambda v: v.astype(jnp.int32)
    return i32(pos), (i32(tile_s), exp_s, i32(lo), i32(hi), i32(first), i32(valid))


def _dispatch_kernel(pos_ref, x_hbm, xs_hbm, sem, *, chunk, n_tok):
    i = pl.program_id(0)
    base = i * chunk
    base_t = base % n_tok

    def row_copy(t, p):
        return pltpu.make_async_copy(x_hbm.at[pl.ds(t, 1), :], xs_hbm.at[pl.ds(p, 1), :], sem)

    def start(j, c):
        row_copy(base_t + j, pos_ref[0, j]).start()
        return c

    def wait(j, c):
        row_copy(base_t + j, pos_ref[0, j]).wait()
        return c

    lax.fori_loop(0, chunk, start, 0, unroll=8)
    lax.fori_loop(0, chunk, wait, 0, unroll=8)


def _dispatch(x, pos, chunk):
    n_tok, d = x.shape
    n_assign = pos.shape[0]
    assert n_tok % chunk == 0
    kern = functools.partial(_dispatch_kernel, chunk=chunk, n_tok=n_tok)
    return pl.pallas_call(
        kern,
        out_shape=jax.ShapeDtypeStruct((n_assign, d), x.dtype),
        grid=(n_assign // chunk,),
        in_specs=[pl.BlockSpec((None, 1, chunk), lambda i: (i, 0, 0), memory_space=pltpu.SMEM),
                  pl.BlockSpec(memory_space=pl.ANY)],
        out_specs=pl.BlockSpec(memory_space=pl.ANY),
        scratch_shapes=[pltpu.SemaphoreType.DMA],
        compiler_params=_cparams(("arbitrary",)),
        name="moe_dispatch",
    )(pos.reshape(n_assign // chunk, 1, chunk), x)


def _gmm_kernel(tile_ref, exp_ref, lo_ref, hi_ref, first_ref, valid_ref, xs_ref, wg_ref, wu_ref, wd_ref,
                o_ref, wg_b, wu_b, wd_b):
    s = pl.program_id(0)
    prev = jnp.maximum(s - 1, 0)
    new_expert = jnp.logical_or(s == 0, exp_ref[s] != exp_ref[prev])

    @pl.when(new_expert)
    def _():
        wg_b[...] = wg_ref[...].astype(BF16)
        wu_b[...] = wu_ref[...].astype(BF16)
        wd_b[...] = wd_ref[...].astype(BF16)

    @pl.when(valid_ref[s] == 1)
    def _():
        xb = xs_ref[...].astype(BF16)
        hid = _silu(jnp.dot(xb, wg_b[...], preferred_element_type=F32)) * \
            jnp.dot(xb, wu_b[...], preferred_element_type=F32)
        y = jnp.dot(hid.astype(BF16), wd_b[...], preferred_element_type=F32)
        rows = lax.broadcasted_iota(jnp.int32, (y.shape[0], 1), 0)
        mine = jnp.logical_and(rows >= lo_ref[s], rows < hi_ref[s])

        @pl.when(first_ref[s] == 1)
        def _():
            o_ref[...] = jnp.where(mine, y, 0.0)

        @pl.when(first_ref[s] == 0)
        def _():
            o_ref[...] = jnp.where(mine, y, o_ref[...])


def _gmm(xs, plan, wg, wu, wd, layer, tile):
    n_rows, d = xs.shape
    n_steps = plan[0].shape[0]
    grid_spec = pltpu.PrefetchScalarGridSpec(
        num_scalar_prefetch=len(plan),
        grid=(n_steps,),
        in_specs=[pl.BlockSpec((tile, d), lambda s, t, e, *_: (t[s], 0)),
                  pl.BlockSpec((None, None, d, D_FF), lambda s, t, e, *_: (layer, e[s], 0, 0)),
                  pl.BlockSpec((None, None, d, D_FF), lambda s, t, e, *_: (layer, e[s], 0, 0)),
                  pl.BlockSpec((None, None, D_FF, d), lambda s, t, e, *_: (layer, e[s], 0, 0))],
        out_specs=pl.BlockSpec((tile, d), lambda s, t, e, *_: (t[s], 0)),
        scratch_shapes=[pltpu.VMEM((d, D_FF), BF16), pltpu.VMEM((d, D_FF), BF16), pltpu.VMEM((D_FF, d), BF16)],
    )
    return pl.pallas_call(
        _gmm_kernel,
        out_shape=jax.ShapeDtypeStruct((n_rows, d), F32),
        grid_spec=grid_spec,
        compiler_params=_cparams(("arbitrary",)),
        name="moe_gmm",
    )(*plan, xs, wg, wu, wd)


def _combine_kernel(pos_ref, posn_ref, w_ref, x_ref, lg_ref, lb_ref, ys_hbm, o_ref, buf, sem, *, tc):
    i = pl.program_id(0)
    n = pl.num_programs(0)
    slot = i % 2

    def row_copy(p_ref, k, r, sl):
        return pltpu.make_async_copy(ys_hbm.at[pl.ds(p_ref[k, r], 1), :], buf.at[sl, k, pl.ds(r, 1), :],
                                     sem.at[sl])

    def start_tile(p_ref, sl):
        for k in range(2):
            lax.fori_loop(0, tc, lambda r, c: (row_copy(p_ref, k, r, sl).start(), c)[1], 0, unroll=8)

    @pl.when(i == 0)
    def _():
        start_tile(pos_ref, 0)

    @pl.when(i + 1 < n)
    def _():
        start_tile(posn_ref, 1 - slot)

    for k in range(2):
        lax.fori_loop(0, tc, lambda r, c: (row_copy(pos_ref, k, r, slot).wait(), c)[1], 0, unroll=8)

    w = w_ref[...]
    sub = lax.broadcasted_iota(jnp.int32, (LANES, LANES), 0)
    for c0 in range(0, tc, LANES):
        wsq = jnp.where(sub == 0, w[0:1, c0:c0 + LANES], jnp.where(sub == 1, w[1:2, c0:c0 + LANES], 0.0))
        wt = wsq.T
        y = wt[:, 0:1] * buf[slot, 0, c0:c0 + LANES, :] + wt[:, 1:2] * buf[slot, 1, c0:c0 + LANES, :]
        o_ref[c0:c0 + LANES, :] = _layer_norm_rows(DN_ALPHA * x_ref[c0:c0 + LANES, :] + y,
                                                   lg_ref[...], lb_ref[...])


def _combine(ys, pos, ew, x, lg, lb, tc):
    n_tok, d = x.shape
    n_t = n_tok // tc
    pos3 = pos.reshape(2, n_t, tc).transpose(1, 0, 2)
    kern = functools.partial(_combine_kernel, tc=tc)
    vec = lambda i: (0, 0)
    return pl.pallas_call(
        kern,
        out_shape=jax.ShapeDtypeStruct((n_tok, d), F32),
        grid=(n_t,),
        in_specs=[pl.BlockSpec((None, 2, tc), lambda i: (i, 0, 0), memory_space=pltpu.SMEM),
                  pl.BlockSpec((None, 2, tc), lambda i: (jnp.minimum(i + 1, n_t - 1), 0, 0),
                               memory_space=pltpu.SMEM),
                  pl.BlockSpec((2, tc), lambda i: (0, i)),
                  pl.BlockSpec((tc, d), lambda i: (i, 0)),
                  pl.BlockSpec((1, d), vec),
                  pl.BlockSpec((1, d), vec),
                  pl.BlockSpec(memory_space=pl.ANY)],
        out_specs=pl.BlockSpec((tc, d), lambda i: (i, 0)),
        scratch_shapes=[pltpu.VMEM((2, 2, tc, d), F32), pltpu.SemaphoreType.DMA((2,))],
        compiler_params=_cparams(("arbitrary",)),
        name="moe_combine",
    )(pos3, pos3, ew, x, lg.reshape(1, -1), lb.reshape(1, -1), ys)


def _moe(x, e_idx, ew, wg, wu, wd, layer, lg, lb):
    pos, plan = _moe_plan(e_idx, MOE_TILE)
    xs = _dispatch(x, pos, MOE_CHUNK)
    ys = _gmm(xs, plan, wg, wu, wd, layer, MOE_TILE)
    return _combine(ys, pos, ew, x, lg, lb, MOE_TILE)


def kernel(x_prompt, x_sample, cache_k, cache_v, page_table, state_conv, state_ret, w_in_a, lambda_a, subln_a,
           conv_w, conv_b, conv_ln_g, conv_ln_b, w_out_a, w_in_c, w_out_c, ln_g, ln_b, w_router, b_router,
           w_e_gate, w_e_up, w_e_down):
    bsz, seq, _ = x_prompt.shape
    dec_b, dec_seq, _ = x_sample.shape
    n_pages = page_table.shape[1]
    past_len = n_pages * PAGE_SIZE
    tp = bsz * seq
    ts_ = dec_b * dec_seq
    tm = min(512, seq)
    assert seq % tm == 0 and ts_ % tm == 0 and dec_seq == SUBLANES
    seq_tiles = seq // tm
    n_prompt_tiles = tp // tm

    pos_p = jnp.arange(seq, dtype=jnp.int32)
    pos_s = past_len + (jnp.arange(tm, dtype=jnp.int32) % dec_seq)
    pos_tab = jnp.concatenate([pos_p, pos_s])
    tabs_a = _rot_tables_a(pos_tab)
    tabs_c = _rot_tables_c(pos_tab)

    n_phys = cache_k.shape[1]
    ck = jnp.transpose(cache_k, (0, 1, 3, 4, 2))
    cv = cache_v.reshape(cache_v.shape[0], n_phys, PAGE_SIZE * H_A, D_VA)
    w_router_t = w_router.T
    w_in_a, w_out_a, w_in_c, w_out_c = (w.astype(BF16) for w in (w_in_a, w_out_a, w_in_c, w_out_c))

    x = jnp.concatenate([x_prompt.reshape(tp, D_MODEL), x_sample.reshape(ts_, D_MODEL)], axis=0)
    k_p, v_p, conv_p, ret_p, k_s, v_s, conv_s, ret_s = [], [], [], [], [], [], [], []
    for l in range(DEPTH):
        if l % 2 == 0:
            ia = l // 2
            lam_init = 0.8 - 0.6 * math.exp(-0.3 * l)
            qkvag = _inproj(x, w_in_a, ia, tabs_a, _inproj_a_kernel, tm, seq_tiles, n_prompt_tiles, "inproj_a")
            att_p = _attn_prompt(qkvag, lambda_a[ia], subln_a[ia], bsz, seq, lam_init, min(256, seq))
            att_s = _attn_sample(qkvag, ck, cv, ia, page_table, lambda_a[ia], subln_a[ia], tp, dec_b,
                                 dec_seq, lam_init)
            c_p, st_p = _conv_prompt(qkvag, conv_w[ia], conv_b[ia], conv_ln_g[ia], conv_ln_b[ia], bsz, seq,
                                     min(256, seq))
            c_s, st_s = _conv_sample(qkvag, state_conv, ia, conv_w[ia], conv_b[ia], conv_ln_g[ia],
                                     conv_ln_b[ia], tp, dec_b, dec_seq, 8)
            att = jnp.concatenate([att_p, att_s], axis=0)
            cc = jnp.concatenate([c_p, c_s], axis=0)
            x, e_idx, ew = _outproj([att, cc], x, w_out_a, ia, ln_g[l, 0], ln_b[l, 0], w_router_t, b_router, tm,
                                  "outproj_a")
            k_all = qkvag[:, QK_W:2 * QK_W]
            v_all = qkvag[:, 2 * QK_W:2 * QK_W + ATT_W]
            k_p.append(k_all[:tp].reshape(bsz, seq, 2 * H_A, D_QK))
            v_p.append(v_all[:tp].reshape(bsz, seq, H_A, D_VA))
            k_s.append(k_all[tp:].reshape(dec_b, dec_seq, 2 * H_A, D_QK))
            v_s.append(v_all[tp:].reshape(dec_b, dec_seq, H_A, D_VA))
            conv_p.append(st_p)
            conv_s.append(st_s)
        else:
            ic = l // 2
            qkvg = _inproj(x, w_in_c, ic, tabs_c, _inproj_c_kernel, tm, seq_tiles, n_prompt_tiles, "inproj_c")
            og_p, s_p = _ret_prompt(qkvg, bsz, seq, min(256, seq))
            og_s, s_s = _ret_sample(qkvg, state_ret, ic, tp, dec_b, dec_seq)
            og = jnp.concatenate([og_p, og_s], axis=0)
            x, e_idx, ew = _outproj([og], x, w_out_c, ic, ln_g[l, 0], ln_b[l, 0], w_router_t, b_router, tm,
                                  "outproj_c")
            ret_p.append(s_p)
            ret_s.append(s_s)
        x = _moe(x, e_idx, ew, w_e_gate, w_e_up, w_e_down, l, ln_g[l, 1], ln_b[l, 1])

    y_prompt = x[:tp].reshape(bsz, seq, D_MODEL)
    y_sample = x[tp:].reshape(dec_b, dec_seq, D_MODEL)
    return (y_prompt, y_sample, jnp.stack(k_p), jnp.stack(v_p), jnp.stack(conv_p), jnp.stack(ret_p),
            jnp.stack(k_s), jnp.stack(v_s), jnp.stack(conv_s), jnp.stack(ret_s))
```

```python
import functools
import math

import jax
import jax.numpy as jnp
from jax import lax
from jax.experimental import pallas as pl
from jax.experimental.pallas import tpu as pltpu

F32 = jnp.float32
BF16 = jnp.bfloat16

D_MODEL = 1024
DEPTH = 4
PAGE_SIZE = 128
H_A = 4
D_QK = 64
D_VA = 2 * D_QK
QK_W = H_A * 2 * D_QK
ATT_W = H_A * D_VA
ROT_DIM = D_QK // 4
ROPE_THETA = 500000.0
CONV_CH = D_MODEL // 2
CONV_W = 31
H_C = 4
DK_C = D_MODEL // H_C
DV_C = 2 * DK_C
RET_THETA = 10000.0
N_EXPERTS = 16
N_GROUPS = 4
EXPERTS_PER_GROUP = N_EXPERTS // N_GROUPS
D_FF = D_MODEL // 2
DN_ALPHA = (2 * DEPTH) ** 0.25
LN_EPS = 1e-5
IN_A = 2 * QK_W + ATT_W + 2 * CONV_CH
IN_C = 2 * H_C * DK_C + 2 * H_C * DV_C

LANES = 128
SUBLANES = 8
CONV_HALO = 32
VMEM_LIMIT = 56 * 1024 * 1024
MOE_TILE = 256


def _cparams(sem):
    return pltpu.CompilerParams(dimension_semantics=sem, vmem_limit_bytes=VMEM_LIMIT)


def _bdot(a, b):
    return jnp.dot(a.astype(BF16), b.astype(BF16), preferred_element_type=F32)


def _bdot_nt(a, b):
    return lax.dot_general(a.astype(BF16), b.astype(BF16), (((1,), (1,)), ((), ())),
                           preferred_element_type=F32)


def _layer_norm_rows(v, g, b):
    mu = jnp.mean(v, -1, keepdims=True)
    d = v - mu
    var = jnp.mean(d * d, -1, keepdims=True)
    return d * lax.rsqrt(var + LN_EPS) * g + b


def _silu(v):
    return v * (1.0 / (1.0 + jnp.exp(-v)))


def _sigmoid(v):
    return 1.0 / (1.0 + jnp.exp(-v))


def _cast_rows_once(x_ref, xb_ref):
    @pl.when(pl.program_id(1) == 0)
    def _():
        xb_ref[...] = x_ref[...].astype(BF16)


def _inproj_a_kernel(x_ref, w_ref, cos_ref, sa_ref, sb_ref, o_ref, xb_ref):
    j = pl.program_id(1)
    _cast_rows_once(x_ref, xb_ref)
    y = jnp.dot(xb_ref[...], w_ref[...], preferred_element_type=F32)

    @pl.when(j < 2)
    def _():
        cos = cos_ref[...]
        sa = sa_ref[...]
        sb = sb_ref[...]
        for blk in range(QK_W // LANES):
            t = y[:, blk * LANES:(blk + 1) * LANES]
            r = (t * cos + pltpu.roll(t, ROT_DIM // 2, 1) * sa
                 + pltpu.roll(t, LANES - ROT_DIM // 2, 1) * sb)
            o_ref[:, blk * LANES:(blk + 1) * LANES] = r

    @pl.when(j >= 2)
    def _():
        o_ref[...] = y


def _rot_tables_a(pos):
    half = ROT_DIM // 2
    inv = ROPE_THETA ** (-jnp.arange(0, ROT_DIM, 2, dtype=F32) / ROT_DIM)
    ang = pos.astype(F32)[:, None] * inv[None, :]
    c, s = jnp.cos(ang), jnp.sin(ang)
    n = pos.shape[0]
    one = jnp.ones((n, D_QK - ROT_DIM), F32)
    zero = jnp.zeros((n, D_QK - ROT_DIM), F32)
    zh = jnp.zeros((n, half), F32)
    cos64 = jnp.concatenate([c, c, one], 1)
    sa64 = jnp.concatenate([zh, s, zero], 1)
    sb64 = jnp.concatenate([-s, zh, zero], 1)
    rep = LANES // D_QK
    return jnp.tile(cos64, (1, rep)), jnp.tile(sa64, (1, rep)), jnp.tile(sb64, (1, rep))


def _inproj_c_kernel(x_ref, w_ref, cos_ref, sin_ref, o_ref, xb_ref):
    j = pl.program_id(1)
    _cast_rows_once(x_ref, xb_ref)
    y = jnp.dot(xb_ref[...], w_ref[...], preferred_element_type=F32)

    @pl.when(j < 4)
    def _():
        cos = cos_ref[...]
        sin = sin_ref[...]
        scale = jnp.where(j >= 2, DK_C ** -0.5, 1.0).astype(F32)
        half = DK_C // 2
        for hd in range(2):
            t1 = y[:, hd * DK_C:hd * DK_C + half]
            t2 = y[:, hd * DK_C + half:(hd + 1) * DK_C]
            o_ref[:, hd * DK_C:hd * DK_C + half] = (t1 * cos - t2 * sin) * scale
            o_ref[:, hd * DK_C + half:(hd + 1) * DK_C] = (t2 * cos + t1 * sin) * scale

    @pl.when(j >= 4)
    def _():
        o_ref[...] = y


def _rot_tables_c(pos):
    inv = RET_THETA ** (-jnp.linspace(0.0, 1.0, DK_C // 2, dtype=F32))
    ang = pos.astype(F32)[:, None] * inv[None, :]
    return jnp.cos(ang), jnp.sin(ang)


def _inproj(x, w, li, tables, kern, tm, seq_tiles, n_prompt_tiles, name):
    t_all, d = x.shape
    n = w.shape[2]
    tn = 512
    tw = tables[0].shape[1]

    def tab_map(i, j):
        return (jnp.where(i < n_prompt_tiles, i % seq_tiles, seq_tiles), 0)

    return pl.pallas_call(
        kern,
        out_shape=jax.ShapeDtypeStruct((t_all, n), F32),
        grid=(t_all // tm, n // tn),
        in_specs=[pl.BlockSpec((tm, d), lambda i, j: (i, 0)),
                  pl.BlockSpec((None, d, tn), lambda i, j: (li, 0, j))]
                 + [pl.BlockSpec((tm, tw), tab_map) for _ in tables],
        out_specs=pl.BlockSpec((tm, tn), lambda i, j: (i, j)),
        scratch_shapes=[pltpu.VMEM((tm, d), BF16)],
        compiler_params=_cparams(("parallel", "arbitrary")),
        name=name,
    )(x, w, *tables)


def _lambda_value(lam_ref, lam_init):
    lf = lam_ref[...]
    a = jnp.sum(lf[0:1, :] * lf[1:2, :], axis=-1, keepdims=True)
    b = jnp.sum(lf[2:3, :] * lf[3:4, :], axis=-1, keepdims=True)
    return jnp.exp(a) - jnp.exp(b) + lam_init


def _sub_norm(o, g, lam_init):
    ms = jnp.mean(o * o, -1, keepdims=True)
    return o * lax.rsqrt(ms + LN_EPS) * g * (1.0 - lam_init)


def _attn_prompt_kernel(q_ref, k_ref, v_ref, lam_ref, g_ref, o_ref, qm_scr, kb_scr, vb_scr, m_scr, acc_scr,
                        *, lam_init, tq):
    qi = pl.program_id(1)
    n_chain = 2 * H_A
    wide = 2 * D_VA

    @pl.when(qi == 0)
    def _():
        kb_scr[...] = k_ref[...].astype(BF16)
        ones = jnp.ones((v_ref.shape[0], D_VA), BF16)
        for h in range(H_A):
            vb_scr[:, h * wide:h * wide + D_VA] = v_ref[:, h * D_VA:(h + 1) * D_VA].astype(BF16)
            vb_scr[:, h * wide + D_VA:(h + 1) * wide] = ones

    lane = lax.broadcasted_iota(jnp.int32, (tq, D_VA), 1)
    for h in range(H_A):
        qh = q_ref[:, h * D_VA:(h + 1) * D_VA] * (D_QK ** -0.5)
        for m in range(2):
            qm_scr[2 * h + m] = jnp.where(lane // D_QK == m, qh, 0.0).astype(BF16)
    m_scr[...] = jnp.full(m_scr.shape, -jnp.inf, F32)
    acc_scr[...] = jnp.zeros(acc_scr.shape, F32)
    row = lax.broadcasted_iota(jnp.int32, (tq, tq), 0)
    col = lax.broadcasted_iota(jnp.int32, (tq, tq), 1)

    def block(j, diagonal):
        start = pl.multiple_of(j * tq, tq)
        for c in range(n_chain):
            h = c // 2
            s = lax.dot_general(qm_scr[c], kb_scr[pl.ds(start, tq), h * D_VA:(h + 1) * D_VA],
                                (((1,), (1,)), ((), ())), preferred_element_type=F32)
            if diagonal:
                s = jnp.where(col <= row, s, -jnp.inf)
            m_prev = m_scr[c]
            m_new = jnp.maximum(m_prev, jnp.max(s, -1, keepdims=True))
            p = jnp.exp(s - m_new).astype(BF16)
            pv = jnp.dot(p, vb_scr[pl.ds(start, tq), h * wide:(h + 1) * wide], preferred_element_type=F32)
            acc_scr[c] = jnp.exp(m_prev - m_new) * acc_scr[c] + pv
            m_scr[c] = m_new

    def trip(j, carry):
        block(j, False)
        return carry

    lax.fori_loop(0, qi, trip, 0)
    block(qi, True)
    lam = _lambda_value(lam_ref, lam_init)
    for h in range(H_A):
        a0 = acc_scr[2 * h]
        a1 = acc_scr[2 * h + 1]
        o = a0[:, :D_VA] / a0[:, D_VA:] - lam * (a1[:, :D_VA] / a1[:, D_VA:])
        o_ref[:, h * D_VA:(h + 1) * D_VA] = _sub_norm(o, g_ref[...], lam_init)


def _attn_prompt(qkvag, lam_p, subln_g, bsz, seq, lam_init, tq):
    nq = seq // tq
    kern = functools.partial(_attn_prompt_kernel, lam_init=lam_init, tq=tq)
    return pl.pallas_call(
        kern,
        out_shape=jax.ShapeDtypeStruct((bsz * seq, ATT_W), F32),
        grid=(bsz, nq),
        in_specs=[pl.BlockSpec((tq, QK_W), lambda b, qi: (b * nq + qi, 0)),
                  pl.BlockSpec((seq, QK_W), lambda b, qi: (b, 1)),
                  pl.BlockSpec((seq, ATT_W), lambda b, qi: (b, 2)),
                  pl.BlockSpec((4, D_QK), lambda b, qi: (0, 0)),
                  pl.BlockSpec((1, D_VA), lambda b, qi: (0, 0))],
        out_specs=pl.BlockSpec((tq, ATT_W), lambda b, qi: (b * nq + qi, 0)),
        scratch_shapes=[pltpu.VMEM((2 * H_A, tq, D_VA), BF16),
                        pltpu.VMEM((seq, QK_W), BF16),
                        pltpu.VMEM((seq, 2 * ATT_W), BF16),
                        pltpu.VMEM((2 * H_A, tq, 1), F32),
                        pltpu.VMEM((2 * H_A, tq, 2 * D_VA), F32)],
        compiler_params=_cparams(("parallel", "arbitrary")),
        name="attn_prompt",
    )(qkvag, qkvag, qkvag, lam_p, subln_g.reshape(1, D_VA))


def _attn_sample_kernel(pt_ref, q_ref, kn_ref, vn_ref, lam_ref, g_ref, *rest, lam_init, n_pages, dec_seq):
    k_refs = rest[:n_pages]
    v_refs = rest[n_pages:2 * n_pages]
    o_ref = rest[2 * n_pages]
    nrow = 2 * H_A * dec_seq
    q = q_ref[...] * (D_QK ** -0.5)
    qt = jnp.concatenate([q] * (2 * H_A), axis=0)
    rid = lax.broadcasted_iota(jnp.int32, (nrow, QK_W), 0)
    cid = lax.broadcasted_iota(jnp.int32, (nrow, QK_W), 1)
    qbd = jnp.where(cid // D_QK == rid // dec_seq, qt, 0.0).astype(BF16)

    s_past = [_bdot(qbd, kr[...].reshape(QK_W, PAGE_SIZE)) for kr in k_refs]
    s_new = _bdot_nt(qbd, kn_ref[...])
    qpos = lax.broadcasted_iota(jnp.int32, (nrow, dec_seq), 0) % dec_seq
    kpos = lax.broadcasted_iota(jnp.int32, (nrow, dec_seq), 1)
    s_new = jnp.where(kpos <= qpos, s_new, -jnp.inf)
    m = jnp.max(s_new, -1, keepdims=True)
    for s in s_past:
        m = jnp.maximum(m, jnp.max(s, -1, keepdims=True))
    p_new = jnp.exp(s_new - m)
    l = jnp.sum(p_new, -1, keepdims=True)
    p_past = []
    for s in s_past:
        p = jnp.exp(s - m)
        l = l + jnp.sum(p, -1, keepdims=True)
        p_past.append(p.astype(BF16))
    lam = _lambda_value(lam_ref, lam_init)
    g = g_ref[...]
    grp = 2 * dec_seq
    for h in range(H_A):
        r0 = h * grp
        acc = _bdot(p_new[r0:r0 + grp], vn_ref[:, h * D_VA:(h + 1) * D_VA])
        for p, vr in zip(p_past, v_refs):
            acc = acc + _bdot(p[r0:r0 + grp], vr[pl.ds(h, PAGE_SIZE, stride=H_A), :])
        acc = acc / l[r0:r0 + grp]
        o = acc[:dec_seq] - lam * acc[dec_seq:]
        o_ref[:, h * D_VA:(h + 1) * D_VA] = _sub_norm(o, g, lam_init)


def _attn_sample(qkvag, cache_kt, cache_vr, ia, page_table, lam_p, subln_g, row0, dec_b, dec_seq, lam_init):
    n_pages = page_table.shape[1]
    kern = functools.partial(_attn_sample_kernel, lam_init=lam_init, n_pages=n_pages, dec_seq=dec_seq)
    blk0 = row0 // dec_seq

    def k_spec(p):
        return pl.BlockSpec((None, None, 2 * H_A, D_QK, PAGE_SIZE), lambda s, pt: (ia, pt[s, p], 0, 0, 0))

    def v_spec(p):
        return pl.BlockSpec((None, None, PAGE_SIZE * H_A, D_VA), lambda s, pt: (ia, pt[s, p], 0, 0))

    grid_spec = pltpu.PrefetchScalarGridSpec(
        num_scalar_prefetch=1,
        grid=(dec_b,),
        in_specs=[pl.BlockSpec((dec_seq, QK_W), lambda s, pt: (blk0 + s, 0)),
                  pl.BlockSpec((dec_seq, QK_W), lambda s, pt: (blk0 + s, 1)),
                  pl.BlockSpec((dec_seq, ATT_W), lambda s, pt: (blk0 + s, 2)),
                  pl.BlockSpec((4, D_QK), lambda s, pt: (0, 0)),
                  pl.BlockSpec((1, D_VA), lambda s, pt: (0, 0))]
                 + [k_spec(p) for p in range(n_pages)]
                 + [v_spec(p) for p in range(n_pages)],
        out_specs=pl.BlockSpec((dec_seq, ATT_W), lambda s, pt: (s, 0)),
    )
    return pl.pallas_call(
        kern,
        out_shape=jax.ShapeDtypeStruct((dec_b * dec_seq, ATT_W), F32),
        grid_spec=grid_spec,
        compiler_params=_cparams(("arbitrary",)),
        name="attn_sample",
    )(page_table, qkvag, qkvag, qkvag, lam_p, subln_g.reshape(1, D_VA),
      *([cache_kt] * n_pages), *([cache_vr] * n_pages))


def _conv_taps(scr, cw_ref, row0, rows):
    acc = None
    for w in range(CONV_W):
        term = scr[pl.ds(row0 + w, rows), :] * cw_ref[w:w + 1, :]
        acc = term if acc is None else acc + term
    return acc


def _conv_prompt_kernel(a_ref, g_ref, ah_ref, gh_ref, cw_ref, cb_ref, lg_ref, lb_ref,
                        c_ref, st_ref, scr, *, ts, chunk):
    i = pl.program_id(1)
    n = pl.num_programs(1)
    u = a_ref[...] * _sigmoid(g_ref[...])
    uh = ah_ref[...] * _sigmoid(gh_ref[...])
    uh = jnp.where(i > 0, uh, 0.0)
    scr[0:CONV_HALO, :] = uh
    scr[CONV_HALO:CONV_HALO + ts, :] = u
    off = CONV_HALO - (CONV_W - 1)
    for c0 in range(0, ts, chunk):
        c = _conv_taps(scr, cw_ref, c0 + off, chunk) + cb_ref[...]
        c_ref[c0:c0 + chunk, :] = _silu(_layer_norm_rows(c, lg_ref[...], lb_ref[...]))

    @pl.when(i == n - 1)
    def _():
        st_ref[...] = scr[CONV_HALO + ts - (CONV_W - 1):CONV_HALO + ts, :]


def _conv_prompt(qkvag, cw, cb, lg, lb, bsz, seq, ts):
    ns = seq // ts
    hb = ts // CONV_HALO
    acol = (2 * QK_W + ATT_W) // CONV_CH
    kern = functools.partial(_conv_prompt_kernel, ts=ts, chunk=32)
    vec = lambda b, i: (0, 0)
    return pl.pallas_call(
        kern,
        out_shape=(jax.ShapeDtypeStruct((bsz * seq, CONV_CH), F32),
                   jax.ShapeDtypeStruct((bsz, CONV_W - 1, CONV_CH), F32)),
        grid=(bsz, ns),
        in_specs=[pl.BlockSpec((ts, CONV_CH), lambda b, i: (b * ns + i, acol)),
                  pl.BlockSpec((ts, CONV_CH), lambda b, i: (b * ns + i, acol + 1)),
                  pl.BlockSpec((CONV_HALO, CONV_CH), lambda b, i: (jnp.maximum((b * ns + i) * hb - 1, 0), acol)),
                  pl.BlockSpec((CONV_HALO, CONV_CH), lambda b, i: (jnp.maximum((b * ns + i) * hb - 1, 0), acol + 1)),
                  pl.BlockSpec((CONV_W, CONV_CH), vec),
                  pl.BlockSpec((1, CONV_CH), vec),
                  pl.BlockSpec((1, CONV_CH), vec),
                  pl.BlockSpec((1, CONV_CH), vec)],
        out_specs=(pl.BlockSpec((ts, CONV_CH), lambda b, i: (b * ns + i, 0)),
                   pl.BlockSpec((None, CONV_W - 1, CONV_CH), lambda b, i: (b, 0, 0))),
        scratch_shapes=[pltpu.VMEM((CONV_HALO + ts, CONV_CH), F32)],
        compiler_params=_cparams(("parallel", "arbitrary")),
        name="conv_prompt",
    )(qkvag, qkvag, qkvag, qkvag, cw, cb.reshape(1, -1), lg.reshape(1, -1), lb.reshape(1, -1))


def _conv_sample_kernel(a_ref, g_ref, st_ref, cw_ref, cb_ref, lg_ref, lb_ref, c_ref, so_ref, scr, *, nb, dec_seq):
    hist = CONV_W - 1
    u = a_ref[...] * _sigmoid(g_ref[...])
    for s in range(nb):
        scr[0:hist, :] = st_ref[s]
        scr[hist:hist + dec_seq, :] = u[s * dec_seq:(s + 1) * dec_seq, :]
        c = _conv_taps(scr, cw_ref, 0, dec_seq) + cb_ref[...]
        c_ref[s * dec_seq:(s + 1) * dec_seq, :] = _silu(_layer_norm_rows(c, lg_ref[...], lb_ref[...]))
        so_ref[s] = scr[dec_seq:dec_seq + hist, :]


def _conv_sample(qkvag, state, ia, cw, cb, lg, lb, row0, dec_b, dec_seq, nb):
    acol = (2 * QK_W + ATT_W) // CONV_CH
    blk0 = row0 // (nb * dec_seq)
    kern = functools.partial(_conv_sample_kernel, nb=nb, dec_seq=dec_seq)
    vec = lambda i: (0, 0)
    return pl.pallas_call(
        kern,
        out_shape=(jax.ShapeDtypeStruct((dec_b * dec_seq, CONV_CH), F32),
                   jax.ShapeDtypeStruct((dec_b, CONV_W - 1, CONV_CH), F32)),
        grid=(dec_b // nb,),
        in_specs=[pl.BlockSpec((nb * dec_seq, CONV_CH), lambda i: (blk0 + i, acol)),
                  pl.BlockSpec((nb * dec_seq, CONV_CH), lambda i: (blk0 + i, acol + 1)),
                  pl.BlockSpec((None, nb, CONV_W - 1, CONV_CH), lambda i: (ia, i, 0, 0)),
                  pl.BlockSpec((CONV_W, CONV_CH), vec),
                  pl.BlockSpec((1, CONV_CH), vec),
                  pl.BlockSpec((1, CONV_CH), vec),
                  pl.BlockSpec((1, CONV_CH), vec)],
        out_specs=(pl.BlockSpec((nb * dec_seq, CONV_CH), lambda i: (i, 0)),
                   pl.BlockSpec((nb, CONV_W - 1, CONV_CH), lambda i: (i, 0, 0))),
        scratch_shapes=[pltpu.VMEM((CONV_W - 1 + dec_seq + SUBLANES, CONV_CH), F32)],
        compiler_params=_cparams(("arbitrary",)),
        name="conv_sample",
    )(qkvag, qkvag, state, cw, cb.reshape(1, -1), lg.reshape(1, -1), lb.reshape(1, -1))


def _ret_tables(c):
    log_g = jnp.log1p(-jnp.exp2(-5.0 - jnp.arange(H_C, dtype=F32)))
    idx = jnp.arange(c, dtype=F32)
    diff = idx[:, None] - idx[None, :]
    causal = diff >= 0
    d_intra = jnp.where(causal[None], jnp.exp(jnp.where(causal, diff, 0.0)[None] * log_g[:, None, None]), 0.0)
    q_dec = jnp.exp((idx[None, :] + 1.0) * log_g[:, None])[..., None]
    k_dec = jnp.exp((c - 1.0 - idx[None, :]) * log_g[:, None])[..., None]
    c_dec = jnp.exp(c * log_g)[:, None, None]
    return d_intra, q_dec, k_dec, c_dec


def _head_norm_gate(o, g):
    mu = jnp.mean(o, -1, keepdims=True)
    d = o - mu
    var = jnp.mean(d * d, -1, keepdims=True)
    return _silu(g) * (d * lax.rsqrt(var + LN_EPS))


def _ret_step(q, k, v, s, d_intra, q_dec, k_dec, c_dec):
    vb = v.astype(BF16)
    sc = _bdot_nt(q, k) * d_intra
    o = jnp.dot(sc.astype(BF16), vb, preferred_element_type=F32) + _bdot(q * q_dec, s)
    s_new = s * c_dec + jnp.dot((k * k_dec).T.astype(BF16), vb, preferred_element_type=F32)
    return o, s_new


def _ret_prompt_kernel(q_ref, k_ref, v_ref, g_ref, di_ref, qd_ref, kd_ref, cd_ref, o_ref, so_ref, s_scr):
    ci = pl.program_id(1)

    @pl.when(ci == 0)
    def _():
        s_scr[...] = jnp.zeros(s_scr.shape, F32)

    for h in range(H_C):
        o, s_new = _ret_step(q_ref[:, h * DK_C:(h + 1) * DK_C], k_ref[:, h * DK_C:(h + 1) * DK_C],
                             v_ref[:, h * DV_C:(h + 1) * DV_C], s_scr[h],
                             di_ref[h], qd_ref[h], kd_ref[h], cd_ref[h])
        s_scr[h] = s_new
        o_ref[:, h * DV_C:(h + 1) * DV_C] = _head_norm_gate(o, g_ref[:, h * DV_C:(h + 1) * DV_C])

    @pl.when(ci == pl.num_programs(1) - 1)
    def _():
        so_ref[...] = s_scr[...]


def _ret_prompt(qkvg, bsz, seq, chunk):
    nc = seq // chunk
    di, qd, kd, cd = _ret_tables(chunk)
    qw = H_C * DK_C
    vw = H_C * DV_C
    whole3 = lambda b, c: (0, 0, 0)
    return pl.pallas_call(
        _ret_prompt_kernel,
        out_shape=(jax.ShapeDtypeStruct((bsz * seq, vw), F32),
                   jax.ShapeDtypeStruct((bsz, H_C, DK_C, DV_C), F32)),
        grid=(bsz, nc),
        in_specs=[pl.BlockSpec((chunk, qw), lambda b, c: (b * nc + c, 0)),
                  pl.BlockSpec((chunk, qw), lambda b, c: (b * nc + c, 1)),
                  pl.BlockSpec((chunk, vw), lambda b, c: (b * nc + c, 1)),
                  pl.BlockSpec((chunk, vw), lambda b, c: (b * nc + c, 2)),
                  pl.BlockSpec((H_C, chunk, chunk), whole3),
                  pl.BlockSpec((H_C, chunk, 1), whole3),
                  pl.BlockSpec((H_C, chunk, 1), whole3),
                  pl.BlockSpec((H_C, 1, 1), whole3)],
        out_specs=(pl.BlockSpec((chunk, vw), lambda b, c: (b * nc + c, 0)),
                   pl.BlockSpec((None, H_C, DK_C, DV_C), lambda b, c: (b, 0, 0, 0))),
        scratch_shapes=[pltpu.VMEM((H_C, DK_C, DV_C), F32)],
        compiler_params=_cparams(("parallel", "arbitrary")),
        name="ret_prompt",
    )(qkvg, qkvg, qkvg, qkvg, di, qd, kd, cd)


def _ret_sample_kernel(q_ref, k_ref, v_ref, g_ref, s_ref, di_ref, qd_ref, kd_ref, cd_ref, o_ref, so_ref):
    for h in range(H_C):
        o, s_new = _ret_step(q_ref[:, h * DK_C:(h + 1) * DK_C], k_ref[:, h * DK_C:(h + 1) * DK_C],
                             v_ref[:, h * DV_C:(h + 1) * DV_C], s_ref[h],
                             di_ref[h], qd_ref[h], kd_ref[h], cd_ref[h])
        so_ref[h] = s_new
        o_ref[:, h * DV_C:(h + 1) * DV_C] = _head_norm_gate(o, g_ref[:, h * DV_C:(h + 1) * DV_C])


def _ret_sample(qkvg, state, ic, row0, dec_b, dec_seq):
    di, qd, kd, cd = _ret_tables(dec_seq)
    blk0 = row0 // dec_seq
    qw = H_C * DK_C
    vw = H_C * DV_C
    whole3 = lambda s: (0, 0, 0)
    return pl.pallas_call(
        _ret_sample_kernel,
        out_shape=(jax.ShapeDtypeStruct((dec_b * dec_seq, vw), F32),
                   jax.ShapeDtypeStruct((dec_b, H_C, DK_C, DV_C), F32)),
        grid=(dec_b,),
        in_specs=[pl.BlockSpec((dec_seq, qw), lambda s: (blk0 + s, 0)),
                  pl.BlockSpec((dec_seq, qw), lambda s: (blk0 + s, 1)),
                  pl.BlockSpec((dec_seq, vw), lambda s: (blk0 + s, 1)),
                  pl.BlockSpec((dec_seq, vw), lambda s: (blk0 + s, 2)),
                  pl.BlockSpec((None, None, H_C, DK_C, DV_C), lambda s: (ic, s, 0, 0, 0)),
                  pl.BlockSpec((H_C, dec_seq, dec_seq), whole3),
                  pl.BlockSpec((H_C, dec_seq, 1), whole3),
                  pl.BlockSpec((H_C, dec_seq, 1), whole3),
                  pl.BlockSpec((H_C, 1, 1), whole3)],
        out_specs=(pl.BlockSpec((dec_seq, vw), lambda s: (s, 0)),
                   pl.BlockSpec((None, H_C, DK_C, DV_C), lambda s: (s, 0, 0, 0))),
        compiler_params=_cparams(("arbitrary",)),
        name="ret_sample",
    )(qkvg, qkvg, qkvg, qkvg, state, di, qd, kd, cd)


def _route(logits):
    rows = [logits[e:e + 1, :] for e in range(N_EXPERTS)]
    mx = functools.reduce(jnp.maximum, rows)
    ex = [jnp.exp(r - mx) for r in rows]
    den = functools.reduce(lambda a, b: a + b, ex)
    probs = [e / den for e in ex]
    return probs


def _pick_top2(probs, bias_ref):
    sel = [probs[e] + bias_ref[e:e + 1, :] for e in range(N_EXPERTS)]
    epg = EXPERTS_PER_GROUP
    neg = jnp.full_like(sel[0], -jnp.inf)
    gscore = []
    for g in range(N_GROUPS):
        grp = sel[g * epg:(g + 1) * epg]
        pairs = [grp[a] + grp[b] for a in range(epg) for b in range(a + 1, epg)]
        gscore.append(functools.reduce(jnp.maximum, pairs))
    best = functools.reduce(jnp.maximum, gscore)
    g_idx = jnp.full(best.shape, N_GROUPS - 1, jnp.int32)
    for g in range(N_GROUPS - 2, -1, -1):
        g_idx = jnp.where(gscore[g] == best, g, g_idx)
    ing = []
    inp = []
    for j in range(epg):
        sv = sel[j]
        pv = probs[j]
        for g in range(1, N_GROUPS):
            sv = jnp.where(g_idx == g, sel[g * epg + j], sv)
            pv = jnp.where(g_idx == g, probs[g * epg + j], pv)
        ing.append(sv)
        inp.append(pv)
    top1 = functools.reduce(jnp.maximum, ing)
    l1 = jnp.full(best.shape, epg - 1, jnp.int32)
    for j in range(epg - 2, -1, -1):
        l1 = jnp.where(ing[j] == top1, j, l1)
    rest = [jnp.where(l1 == j, neg, ing[j]) for j in range(epg)]
    top2 = functools.reduce(jnp.maximum, rest)
    l2 = jnp.full(best.shape, epg - 1, jnp.int32)
    for j in range(epg - 2, -1, -1):
        l2 = jnp.where(jnp.logical_and(rest[j] == top2, l1 != j), j, l2)
    p1 = functools.reduce(lambda a, b: a + b, [jnp.where(l1 == j, inp[j], 0.0) for j in range(epg)])
    p2 = functools.reduce(lambda a, b: a + b, [jnp.where(l2 == j, inp[j], 0.0) for j in range(epg)])
    tot = p1 + p2
    return g_idx * epg + l1, g_idx * epg + l2, p1 / tot, p2 / tot


def _outproj_kernel(*refs, n_in):
    in_refs = refs[:n_in]
    x_ref, w_ref, lg_ref, lb_ref, wr_ref, br_ref, o_ref, eidx_ref, ew_ref = refs[n_in:]
    a = in_refs[0][...] if n_in == 1 else jnp.concatenate([r[...] for r in in_refs], axis=1)
    mix = _bdot(a, w_ref[...])
    x1 = _layer_norm_rows(DN_ALPHA * x_ref[...] + mix, lg_ref[...], lb_ref[...])
    o_ref[...] = x1
    logits = lax.dot_general(wr_ref[...], x1, (((1,), (1,)), ((), ())),
                             precision=lax.Precision.HIGHEST, preferred_element_type=F32)
    probs = _route(logits)
    e1, e2, w1, w2 = _pick_top2(probs, br_ref)
    eidx_ref[0:1, :] = e1
    eidx_ref[1:2, :] = e2
    ew_ref[0:1, :] = w1
    ew_ref[1:2, :] = w2


def _outproj(parts, x, w, li, lg, lb, w_router_t, b_router, tm, name):
    t_all = x.shape[0]
    kern = functools.partial(_outproj_kernel, n_in=len(parts))
    vec = lambda i: (0, 0)
    return pl.pallas_call(
        kern,
        out_shape=(jax.ShapeDtypeStruct((t_all, D_MODEL), F32),
                   jax.ShapeDtypeStruct((2, t_all), jnp.int32),
                   jax.ShapeDtypeStruct((2, t_all), F32)),
        grid=(t_all // tm,),
        in_specs=[pl.BlockSpec((tm, p.shape[1]), lambda i: (i, 0)) for p in parts]
                 + [pl.BlockSpec((tm, D_MODEL), lambda i: (i, 0)),
                    pl.BlockSpec((None,) + w.shape[1:], lambda i: (li, 0, 0)),
                    pl.BlockSpec((1, D_MODEL), vec),
                    pl.BlockSpec((1, D_MODEL), vec),
                    pl.BlockSpec((N_EXPERTS, D_MODEL), vec),
                    pl.BlockSpec((N_EXPERTS, 1), vec)],
        out_specs=(pl.BlockSpec((tm, D_MODEL), lambda i: (i, 0)),
                   pl.BlockSpec((2, tm), lambda i: (0, i)),
                   pl.BlockSpec((2, tm), lambda i: (0, i))),
        compiler_params=_cparams(("parallel",)),
        name=name,
    )(*parts, x, w, lg.reshape(1, -1), lb.reshape(1, -1), w_router_t, b_router.reshape(-1, 1))


def _moe_plan(e_idx, tile):
    n_assign = e_idx.size
    e = e_idx.reshape(n_assign)
    onehot = (e[:, None] == jnp.arange(N_EXPERTS, dtype=jnp.int32)[None, :]).astype(jnp.int32)
    csum = jnp.cumsum(onehot, axis=0)
    counts = csum[-1]
    ends = jnp.cumsum(counts)
    starts = ends - counts
    pos = jnp.sum(onehot * (starts[None, :] + csum - 1), axis=1)
    n_tiles = n_assign // tile
    n_steps = n_tiles + N_EXPERTS - 1
    first_tile = starts // tile
    tiles_e = jnp.where(counts > 0, (ends + tile - 1) // tile - first_tile, 0)
    step_end = jnp.cumsum(tiles_e)
    step_start = step_end - tiles_e
    total = step_end[-1]
    sidx = jnp.arange(n_steps, dtype=jnp.int32)
    valid = sidx < total
    s_eff = jnp.minimum(sidx, total - 1)
    exp_s = jnp.sum((step_end[None, :] <= s_eff[:, None]).astype(jnp.int32), axis=1)
    exp_s = jnp.minimum(exp_s, N_EXPERTS - 1)
    tile_s = first_tile[exp_s] + (s_eff - step_start[exp_s])
    lo = jnp.maximum(starts[exp_s], tile_s * tile) - tile_s * tile
    hi = jnp.minimum(ends[exp_s], (tile_s + 1) * tile) - tile_s * tile
    prev_tile = jnp.concatenate([jnp.full((1,), -1, jnp.int32), tile_s[:-1]])
    first = jnp.logical_and(valid, tile_s != prev_tile)
    i32 = lambda v: v.astype(jnp.int32)
    return i32(pos), (i32(tile_s), exp_s, i32(lo), i32(hi), i32(first), i32(valid))


def _dispatch_kernel(pos_ref, x_ref, xs_hbm, sem, *, tc):
    def row_copy(k, r):
        return pltpu.make_async_copy(x_ref.at[pl.ds(r, 1), :], xs_hbm.at[pl.ds(pos_ref[k, r], 1), :], sem)

    for k in range(2):
        lax.fori_loop(0, tc, lambda r, c: (row_copy(k, r).start(), c)[1], 0, unroll=8)
    for k in range(2):
        lax.fori_loop(0, tc, lambda r, c: (row_copy(k, r).wait(), c)[1], 0, unroll=8)


def _dispatch(x, pos3, tc):
    n_tok, d = x.shape
    kern = functools.partial(_dispatch_kernel, tc=tc)
    return pl.pallas_call(
        kern,
        out_shape=jax.ShapeDtypeStruct((2 * n_tok, d), x.dtype),
        grid=(n_tok // tc,),
        in_specs=[pl.BlockSpec((None, 2, tc), lambda i: (i, 0, 0), memory_space=pltpu.SMEM),
                  pl.BlockSpec((tc, d), lambda i: (i, 0))],
        out_specs=pl.BlockSpec(memory_space=pl.ANY),
        scratch_shapes=[pltpu.SemaphoreType.DMA],
        compiler_params=_cparams(("arbitrary",)),
        name="moe_dispatch",
    )(pos3, x)


def _gmm_kernel(tile_ref, exp_ref, lo_ref, hi_ref, first_ref, valid_ref, xs_ref, wg_ref, wu_ref, wd_ref,
                o_ref, wg_b, wu_b, wd_b):
    s = pl.program_id(0)
    prev = jnp.maximum(s - 1, 0)
    new_expert = jnp.logical_or(s == 0, exp_ref[s] != exp_ref[prev])

    @pl.when(new_expert)
    def _():
        wg_b[...] = wg_ref[...].astype(BF16)
        wu_b[...] = wu_ref[...].astype(BF16)
        wd_b[...] = wd_ref[...].astype(BF16)

    @pl.when(valid_ref[s] == 1)
    def _():
        xb = xs_ref[...].astype(BF16)
        hid = _silu(jnp.dot(xb, wg_b[...], preferred_element_type=F32)) * \
            jnp.dot(xb, wu_b[...], preferred_element_type=F32)
        y = jnp.dot(hid.astype(BF16), wd_b[...], preferred_element_type=F32)
        rows = lax.broadcasted_iota(jnp.int32, (y.shape[0], 1), 0)
        mine = jnp.logical_and(rows >= lo_ref[s], rows < hi_ref[s])

        @pl.when(first_ref[s] == 1)
        def _():
            o_ref[...] = jnp.where(mine, y, 0.0)

        @pl.when(first_ref[s] == 0)
        def _():
            o_ref[...] = jnp.where(mine, y, o_ref[...])


def _gmm(xs, plan, wg, wu, wd, layer, tile):
    n_rows, d = xs.shape
    n_steps = plan[0].shape[0]
    grid_spec = pltpu.PrefetchScalarGridSpec(
        num_scalar_prefetch=len(plan),
        grid=(n_steps,),
        in_specs=[pl.BlockSpec((tile, d), lambda s, t, e, *_: (t[s], 0)),
                  pl.BlockSpec((None, None, d, D_FF), lambda s, t, e, *_: (layer, e[s], 0, 0)),
                  pl.BlockSpec((None, None, d, D_FF), lambda s, t, e, *_: (layer, e[s], 0, 0)),
                  pl.BlockSpec((None, None, D_FF, d), lambda s, t, e, *_: (layer, e[s], 0, 0))],
        out_specs=pl.BlockSpec((tile, d), lambda s, t, e, *_: (t[s], 0)),
        scratch_shapes=[pltpu.VMEM((d, D_FF), BF16), pltpu.VMEM((d, D_FF), BF16), pltpu.VMEM((D_FF, d), BF16)],
    )
    return pl.pallas_call(
        _gmm_kernel,
        out_shape=jax.ShapeDtypeStruct((n_rows, d), F32),
        grid_spec=grid_spec,
        compiler_params=_cparams(("arbitrary",)),
        name="moe_gmm",
    )(*plan, xs, wg, wu, wd)


def _combine_kernel(pos_ref, posn_ref, w_ref, x_ref, lg_ref, lb_ref, ys_hbm, o_ref, buf, sem, *, tc):
    i = pl.program_id(0)
    n = pl.num_programs(0)
    slot = i % 2

    def row_copy(p_ref, k, r, sl):
        return pltpu.make_async_copy(ys_hbm.at[pl.ds(p_ref[k, r], 1), :], buf.at[sl, k, pl.ds(r, 1), :],
                                     sem.at[sl])

    def start_tile(p_ref, sl):
        for k in range(2):
            lax.fori_loop(0, tc, lambda r, c: (row_copy(p_ref, k, r, sl).start(), c)[1], 0, unroll=8)

    @pl.when(i == 0)
    def _():
        start_tile(pos_ref, 0)

    @pl.when(i + 1 < n)
    def _():
        start_tile(posn_ref, 1 - slot)

    for k in range(2):
        lax.fori_loop(0, tc, lambda r, c: (row_copy(pos_ref, k, r, slot).wait(), c)[1], 0, unroll=8)

    w = w_ref[...]
    sub = lax.broadcasted_iota(jnp.int32, (LANES, LANES), 0)
    for c0 in range(0, tc, LANES):
        wsq = jnp.where(sub == 0, w[0:1, c0:c0 + LANES], jnp.where(sub == 1, w[1:2, c0:c0 + LANES], 0.0))
        wt = wsq.T
        y = wt[:, 0:1] * buf[slot, 0, c0:c0 + LANES, :] + wt[:, 1:2] * buf[slot, 1, c0:c0 + LANES, :]
        o_ref[c0:c0 + LANES, :] = _layer_norm_rows(DN_ALPHA * x_ref[c0:c0 + LANES, :] + y,
                                                   lg_ref[...], lb_ref[...])


def _combine(ys, pos3, ew, x, lg, lb, tc):
    n_tok, d = x.shape
    n_t = n_tok // tc
    kern = functools.partial(_combine_kernel, tc=tc)
    vec = lambda i: (0, 0)
    return pl.pallas_call(
        kern,
        out_shape=jax.ShapeDtypeStruct((n_tok, d), F32),
        grid=(n_t,),
        in_specs=[pl.BlockSpec((None, 2, tc), lambda i: (i, 0, 0), memory_space=pltpu.SMEM),
                  pl.BlockSpec((None, 2, tc), lambda i: (jnp.minimum(i + 1, n_t - 1), 0, 0),
                               memory_space=pltpu.SMEM),
                  pl.BlockSpec((2, tc), lambda i: (0, i)),
                  pl.BlockSpec((tc, d), lambda i: (i, 0)),
                  pl.BlockSpec((1, d), vec),
                  pl.BlockSpec((1, d), vec),
                  pl.BlockSpec(memory_space=pl.ANY)],
        out_specs=pl.BlockSpec((tc, d), lambda i: (i, 0)),
        scratch_shapes=[pltpu.VMEM((2, 2, tc, d), F32), pltpu.SemaphoreType.DMA((2,))],
        compiler_params=_cparams(("arbitrary",)),
        name="moe_combine",
    )(pos3, pos3, ew, x, lg.reshape(1, -1), lb.reshape(1, -1), ys)


def _moe(x, e_idx, ew, wg, wu, wd, layer, lg, lb):
    pos, plan = _moe_plan(e_idx, MOE_TILE)
    n_t = x.shape[0] // MOE_TILE
    pos3 = pos.reshape(2, n_t, MOE_TILE).transpose(1, 0, 2)
    xs = _dispatch(x, pos3, MOE_TILE)
    ys = _gmm(xs, plan, wg, wu, wd, layer, MOE_TILE)
    return _combine(ys, pos3, ew, x, lg, lb, MOE_TILE)


def kernel(x_prompt, x_sample, cache_k, cache_v, page_table, state_conv, state_ret, w_in_a, lambda_a, subln_a,
           conv_w, conv_b, conv_ln_g, conv_ln_b, w_out_a, w_in_c, w_out_c, ln_g, ln_b, w_router, b_router,
           w_e_gate, w_e_up, w_e_down):
    bsz, seq, _ = x_prompt.shape
    dec_b, dec_seq, _ = x_sample.shape
    n_pages = page_table.shape[1]
    past_len = n_pages * PAGE_SIZE
    tp = bsz * seq
    ts_ = dec_b * dec_seq
    tm = min(512, seq)
    assert seq % tm == 0 and ts_ % tm == 0 and dec_seq == SUBLANES
    seq_tiles = seq // tm
    n_prompt_tiles = tp // tm

    pos_p = jnp.arange(seq, dtype=jnp.int32)
    pos_s = past_len + (jnp.arange(tm, dtype=jnp.int32) % dec_seq)
    pos_tab = jnp.concatenate([pos_p, pos_s])
    tabs_a = _rot_tables_a(pos_tab)
    tabs_c = _rot_tables_c(pos_tab)

    n_phys = cache_k.shape[1]
    ck = jnp.transpose(cache_k, (0, 1, 3, 4, 2))
    cv = cache_v.reshape(cache_v.shape[0], n_phys, PAGE_SIZE * H_A, D_VA)
    w_router_t = w_router.T
    w_in_a, w_out_a, w_in_c, w_out_c = (w.astype(BF16) for w in (w_in_a, w_out_a, w_in_c, w_out_c))

    x = jnp.concatenate([x_prompt.reshape(tp, D_MODEL), x_sample.reshape(ts_, D_MODEL)], axis=0)
    k_p, v_p, conv_p, ret_p, k_s, v_s, conv_s, ret_s = [], [], [], [], [], [], [], []
    for l in range(DEPTH):
        if l % 2 == 0:
            ia = l // 2
            lam_init = 0.8 - 0.6 * math.exp(-0.3 * l)
            qkvag = _inproj(x, w_in_a, ia, tabs_a, _inproj_a_kernel, tm, seq_tiles, n_prompt_tiles, "inproj_a")
            att_p = _attn_prompt(qkvag, lambda_a[ia], subln_a[ia], bsz, seq, lam_init, min(256, seq))
            att_s = _attn_sample(qkvag, ck, cv, ia, page_table, lambda_a[ia], subln_a[ia], tp, dec_b,
                                 dec_seq, lam_init)
            c_p, st_p = _conv_prompt(qkvag, conv_w[ia], conv_b[ia], conv_ln_g[ia], conv_ln_b[ia], bsz, seq,
                                     min(256, seq))
            c_s, st_s = _conv_sample(qkvag, state_conv, ia, conv_w[ia], conv_b[ia], conv_ln_g[ia],
                                     conv_ln_b[ia], tp, dec_b, dec_seq, 8)
            att = jnp.concatenate([att_p, att_s], axis=0)
            cc = jnp.concatenate([c_p, c_s], axis=0)
            x, e_idx, ew = _outproj([att, cc], x, w_out_a, ia, ln_g[l, 0], ln_b[l, 0], w_router_t, b_router, tm,
                                  "outproj_a")
            k_all = qkvag[:, QK_W:2 * QK_W]
            v_all = qkvag[:, 2 * QK_W:2 * QK_W + ATT_W]
            k_p.append(k_all[:tp].reshape(bsz, seq, 2 * H_A, D_QK))
            v_p.append(v_all[:tp].reshape(bsz, seq, H_A, D_VA))
            k_s.append(k_all[tp:].reshape(dec_b, dec_seq, 2 * H_A, D_QK))
            v_s.append(v_all[tp:].reshape(dec_b, dec_seq, H_A, D_VA))
            conv_p.append(st_p)
            conv_s.append(st_s)
        else:
            ic = l // 2
            qkvg = _inproj(x, w_in_c, ic, tabs_c, _inproj_c_kernel, tm, seq_tiles, n_prompt_tiles, "inproj_c")
            og_p, s_p = _ret_prompt(qkvg, bsz, seq, min(256, seq))
            og_s, s_s = _ret_sample(qkvg, state_ret, ic, tp, dec_b, dec_seq)
            og = jnp.concatenate([og_p, og_s], axis=0)
            x, e_idx, ew = _outproj([og], x, w_out_c, ic, ln_g[l, 0], ln_b[l, 0], w_router_t, b_router, tm,
                                  "outproj_c")
            ret_p.append(s_p)
            ret_s.append(s_s)
        x = _moe(x, e_idx, ew, w_e_gate, w_e_up, w_e_down, l, ln_g[l, 1], ln_b[l, 1])

    y_prompt = x[:tp].reshape(bsz, seq, D_MODEL)
    y_sample = x[tp:].reshape(dec_b, dec_seq, D_MODEL)
    return (y_prompt, y_sample, jnp.stack(k_p), jnp.stack(v_p), jnp.stack(conv_p), jnp.stack(ret_p),
            jnp.stack(k_s), jnp.stack(v_s), jnp.stack(conv_s), jnp.stack(ret_s))
```

```python
import functools
import math

import jax
import jax.numpy as jnp
from jax import lax
from jax.experimental import pallas as pl
from jax.experimental.pallas import tpu as pltpu

F32 = jnp.float32
BF16 = jnp.bfloat16

D_MODEL = 1024
DEPTH = 4
PAGE_SIZE = 128
H_A = 4
D_QK = 64
D_VA = 2 * D_QK
QK_W = H_A * 2 * D_QK
ATT_W = H_A * D_VA
ROT_DIM = D_QK // 4
ROPE_THETA = 500000.0
CONV_CH = D_MODEL // 2
CONV_W = 31
H_C = 4
DK_C = D_MODEL // H_C
DV_C = 2 * DK_C
RET_THETA = 10000.0
N_EXPERTS = 16
N_GROUPS = 4
EXPERTS_PER_GROUP = N_EXPERTS // N_GROUPS
D_FF = D_MODEL // 2
DN_ALPHA = (2 * DEPTH) ** 0.25
LN_EPS = 1e-5
IN_A = 2 * QK_W + ATT_W + 2 * CONV_CH
IN_C = 2 * H_C * DK_C + 2 * H_C * DV_C

LANES = 128
SUBLANES = 8
CONV_HALO = 32
VMEM_LIMIT = 56 * 1024 * 1024
MOE_TILE = 256


def _cparams(sem):
    return pltpu.CompilerParams(dimension_semantics=sem, vmem_limit_bytes=VMEM_LIMIT)


def _bdot(a, b):
    return jnp.dot(a.astype(BF16), b.astype(BF16), preferred_element_type=F32)


def _bdot_nt(a, b):
    return lax.dot_general(a.astype(BF16), b.astype(BF16), (((1,), (1,)), ((), ())),
                           preferred_element_type=F32)


def _layer_norm_rows(v, g, b):
    mu = jnp.mean(v, -1, keepdims=True)
    d = v - mu
    var = jnp.mean(d * d, -1, keepdims=True)
    return d * lax.rsqrt(var + LN_EPS) * g + b


def _silu(v):
    return v * (1.0 / (1.0 + jnp.exp(-v)))


def _sigmoid(v):
    return 1.0 / (1.0 + jnp.exp(-v))


def _cast_rows_once(x_ref, xb_ref):
    @pl.when(pl.program_id(1) == 0)
    def _():
        xb_ref[...] = x_ref[...].astype(BF16)


def _inproj_a_kernel(x_ref, w_ref, cos_ref, sa_ref, sb_ref, o_ref, xb_ref):
    j = pl.program_id(1)
    _cast_rows_once(x_ref, xb_ref)
    y = jnp.dot(xb_ref[...], w_ref[...], preferred_element_type=F32)

    @pl.when(j < 2)
    def _():
        cos = cos_ref[...]
        sa = sa_ref[...]
        sb = sb_ref[...]
        for blk in range(QK_W // LANES):
            t = y[:, blk * LANES:(blk + 1) * LANES]
            r = (t * cos + pltpu.roll(t, ROT_DIM // 2, 1) * sa
                 + pltpu.roll(t, LANES - ROT_DIM // 2, 1) * sb)
            o_ref[:, blk * LANES:(blk + 1) * LANES] = r

    @pl.when(j >= 2)
    def _():
        o_ref[...] = y


def _rot_tables_a(pos):
    half = ROT_DIM // 2
    inv = ROPE_THETA ** (-jnp.arange(0, ROT_DIM, 2, dtype=F32) / ROT_DIM)
    ang = pos.astype(F32)[:, None] * inv[None, :]
    c, s = jnp.cos(ang), jnp.sin(ang)
    n = pos.shape[0]
    one = jnp.ones((n, D_QK - ROT_DIM), F32)
    zero = jnp.zeros((n, D_QK - ROT_DIM), F32)
    zh = jnp.zeros((n, half), F32)
    cos64 = jnp.concatenate([c, c, one], 1)
    sa64 = jnp.concatenate([zh, s, zero], 1)
    sb64 = jnp.concatenate([-s, zh, zero], 1)
    rep = LANES // D_QK
    return jnp.tile(cos64, (1, rep)), jnp.tile(sa64, (1, rep)), jnp.tile(sb64, (1, rep))


def _inproj_c_kernel(x_ref, w_ref, cos_ref, sin_ref, o_ref, xb_ref):
    j = pl.program_id(1)
    _cast_rows_once(x_ref, xb_ref)
    y = jnp.dot(xb_ref[...], w_ref[...], preferred_element_type=F32)

    @pl.when(j < 4)
    def _():
        cos = cos_ref[...]
        sin = sin_ref[...]
        scale = jnp.where(j >= 2, DK_C ** -0.5, 1.0).astype(F32)
        half = DK_C // 2
        for hd in range(2):
            t1 = y[:, hd * DK_C:hd * DK_C + half]
            t2 = y[:, hd * DK_C + half:(hd + 1) * DK_C]
            o_ref[:, hd * DK_C:hd * DK_C + half] = (t1 * cos - t2 * sin) * scale
            o_ref[:, hd * DK_C + half:(hd + 1) * DK_C] = (t2 * cos + t1 * sin) * scale

    @pl.when(j >= 4)
    def _():
        o_ref[...] = y


def _rot_tables_c(pos):
    inv = RET_THETA ** (-jnp.linspace(0.0, 1.0, DK_C // 2, dtype=F32))
    ang = pos.astype(F32)[:, None] * inv[None, :]
    return jnp.cos(ang), jnp.sin(ang)


def _inproj(x, w, li, tables, kern, tm, seq_tiles, n_prompt_tiles, name):
    t_all, d = x.shape
    n = w.shape[2]
    tn = 512
    tw = tables[0].shape[1]

    def tab_map(i, j):
        return (jnp.where(i < n_prompt_tiles, i % seq_tiles, seq_tiles), 0)

    return pl.pallas_call(
        kern,
        out_shape=jax.ShapeDtypeStruct((t_all, n), F32),
        grid=(t_all // tm, n // tn),
        in_specs=[pl.BlockSpec((tm, d), lambda i, j: (i, 0)),
                  pl.BlockSpec((None, d, tn), lambda i, j: (li, 0, j))]
                 + [pl.BlockSpec((tm, tw), tab_map) for _ in tables],
        out_specs=pl.BlockSpec((tm, tn), lambda i, j: (i, j)),
        scratch_shapes=[pltpu.VMEM((tm, d), BF16)],
        compiler_params=_cparams(("parallel", "arbitrary")),
        name=name,
    )(x, w, *tables)


def _lambda_value(lam_ref, lam_init):
    lf = lam_ref[...]
    a = jnp.sum(lf[0:1, :] * lf[1:2, :], axis=-1, keepdims=True)
    b = jnp.sum(lf[2:3, :] * lf[3:4, :], axis=-1, keepdims=True)
    return jnp.exp(a) - jnp.exp(b) + lam_init


def _sub_norm(o, g, lam_init):
    ms = jnp.mean(o * o, -1, keepdims=True)
    return o * lax.rsqrt(ms + LN_EPS) * g * (1.0 - lam_init)


def _attn_prompt_kernel(q_ref, k_ref, v_ref, lam_ref, g_ref, o_ref, qt_scr, kb_scr, vt_scr, m_scr, acc_scr,
                        *, lam_init, tq):
    qi = pl.program_id(1)
    n_chain = 2 * H_A
    n_kt = k_ref.shape[0] // tq

    @pl.when(qi == 0)
    def _():
        kb_scr[...] = k_ref[...].astype(BF16)
        ones = jnp.ones((D_VA, tq), BF16)
        for j in range(n_kt):
            for h in range(H_A):
                vt_scr[j, h, 0:D_VA, :] = v_ref[j * tq:(j + 1) * tq, h * D_VA:(h + 1) * D_VA].T.astype(BF16)
                vt_scr[j, h, D_VA:2 * D_VA, :] = ones

    sub = lax.broadcasted_iota(jnp.int32, (D_VA, tq), 0)
    for h in range(H_A):
        qht = (q_ref[:, h * D_VA:(h + 1) * D_VA] * (D_QK ** -0.5)).T
        for m in range(2):
            qt_scr[2 * h + m] = jnp.where(sub // D_QK == m, qht, 0.0).astype(BF16)
    m_scr[...] = jnp.full(m_scr.shape, -jnp.inf, F32)
    acc_scr[...] = jnp.zeros(acc_scr.shape, F32)
    key = lax.broadcasted_iota(jnp.int32, (tq, tq), 0)
    qry = lax.broadcasted_iota(jnp.int32, (tq, tq), 1)

    def block(j, diagonal):
        start = pl.multiple_of(j * tq, tq)
        for c in range(n_chain):
            h = c // 2
            st = jnp.dot(kb_scr[pl.ds(start, tq), h * D_VA:(h + 1) * D_VA], qt_scr[c],
                         preferred_element_type=F32)
            if diagonal:
                st = jnp.where(key <= qry, st, -jnp.inf)
            m_prev = m_scr[c]
            m_new = jnp.maximum(m_prev, jnp.max(st, 0, keepdims=True))
            pt = jnp.exp(st - m_new).astype(BF16)
            pv = jnp.dot(vt_scr[j, h], pt, preferred_element_type=F32)
            acc_scr[c] = jnp.exp(m_prev - m_new) * acc_scr[c] + pv
            m_scr[c] = m_new

    def trip(j, carry):
        block(j, False)
        return carry

    lax.fori_loop(0, qi, trip, 0)
    block(qi, True)
    lam = _lambda_value(lam_ref, lam_init)
    for h in range(H_A):
        a0 = acc_scr[2 * h]
        a1 = acc_scr[2 * h + 1]
        ot = a0[:D_VA] / a0[D_VA:D_VA + 1] - lam * (a1[:D_VA] / a1[D_VA:D_VA + 1])
        o_ref[:, h * D_VA:(h + 1) * D_VA] = _sub_norm(ot.T, g_ref[...], lam_init)


def _attn_prompt(qkvag, lam_p, subln_g, bsz, seq, lam_init, tq):
    nq = seq // tq
    kern = functools.partial(_attn_prompt_kernel, lam_init=lam_init, tq=tq)
    return pl.pallas_call(
        kern,
        out_shape=jax.ShapeDtypeStruct((bsz * seq, ATT_W), F32),
        grid=(bsz, nq),
        in_specs=[pl.BlockSpec((tq, QK_W), lambda b, qi: (b * nq + qi, 0)),
                  pl.BlockSpec((seq, QK_W), lambda b, qi: (b, 1)),
                  pl.BlockSpec((seq, ATT_W), lambda b, qi: (b, 2)),
                  pl.BlockSpec((4, D_QK), lambda b, qi: (0, 0)),
                  pl.BlockSpec((1, D_VA), lambda b, qi: (0, 0))],
        out_specs=pl.BlockSpec((tq, ATT_W), lambda b, qi: (b * nq + qi, 0)),
        scratch_shapes=[pltpu.VMEM((2 * H_A, D_VA, tq), BF16),
                        pltpu.VMEM((seq, QK_W), BF16),
                        pltpu.VMEM((nq, H_A, 2 * D_VA, tq), BF16),
                        pltpu.VMEM((2 * H_A, 1, tq), F32),
                        pltpu.VMEM((2 * H_A, 2 * D_VA, tq), F32)],
        compiler_params=_cparams(("parallel", "arbitrary")),
        name="attn_prompt",
    )(qkvag, qkvag, qkvag, lam_p, subln_g.reshape(1, D_VA))


def _attn_sample_kernel(pt_ref, q_ref, kn_ref, vn_ref, lam_ref, g_ref, *rest, lam_init, n_pages, dec_seq):
    k_refs = rest[:n_pages]
    v_refs = rest[n_pages:2 * n_pages]
    o_ref = rest[2 * n_pages]
    nrow = 2 * H_A * dec_seq
    q = q_ref[...] * (D_QK ** -0.5)
    qt = jnp.concatenate([q] * (2 * H_A), axis=0)
    rid = lax.broadcasted_iota(jnp.int32, (nrow, QK_W), 0)
    cid = lax.broadcasted_iota(jnp.int32, (nrow, QK_W), 1)
    qbd = jnp.where(cid // D_QK == rid // dec_seq, qt, 0.0).astype(BF16)

    s_past = [_bdot(qbd, kr[...].reshape(QK_W, PAGE_SIZE)) for kr in k_refs]
    s_new = _bdot_nt(qbd, kn_ref[...])
    qpos = lax.broadcasted_iota(jnp.int32, (nrow, dec_seq), 0) % dec_seq
    kpos = lax.broadcasted_iota(jnp.int32, (nrow, dec_seq), 1)
    s_new = jnp.where(kpos <= qpos, s_new, -jnp.inf)
    m = jnp.max(s_new, -1, keepdims=True)
    for s in s_past:
        m = jnp.maximum(m, jnp.max(s, -1, keepdims=True))
    p_new = jnp.exp(s_new - m)
    l = jnp.sum(p_new, -1, keepdims=True)
    p_past = []
    for s in s_past:
        p = jnp.exp(s - m)
        l = l + jnp.sum(p, -1, keepdims=True)
        p_past.append(p.astype(BF16))
    lam = _lambda_value(lam_ref, lam_init)
    g = g_ref[...]
    grp = 2 * dec_seq
    for h in range(H_A):
        r0 = h * grp
        acc = _bdot(p_new[r0:r0 + grp], vn_ref[:, h * D_VA:(h + 1) * D_VA])
        for p, vr in zip(p_past, v_refs):
            acc = acc + _bdot(p[r0:r0 + grp], vr[pl.ds(h, PAGE_SIZE, stride=H_A), :])
        acc = acc / l[r0:r0 + grp]
        o = acc[:dec_seq] - lam * acc[dec_seq:]
        o_ref[:, h * D_VA:(h + 1) * D_VA] = _sub_norm(o, g, lam_init)


def _attn_sample(qkvag, cache_kt, cache_vr, ia, page_table, lam_p, subln_g, row0, dec_b, dec_seq, lam_init):
    n_pages = page_table.shape[1]
    kern = functools.partial(_attn_sample_kernel, lam_init=lam_init, n_pages=n_pages, dec_seq=dec_seq)
    blk0 = row0 // dec_seq

    def k_spec(p):
        return pl.BlockSpec((None, None, 2 * H_A, D_QK, PAGE_SIZE), lambda s, pt: (ia, pt[s, p], 0, 0, 0))

    def v_spec(p):
        return pl.BlockSpec((None, None, PAGE_SIZE * H_A, D_VA), lambda s, pt: (ia, pt[s, p], 0, 0))

    grid_spec = pltpu.PrefetchScalarGridSpec(
        num_scalar_prefetch=1,
        grid=(dec_b,),
        in_specs=[pl.BlockSpec((dec_seq, QK_W), lambda s, pt: (blk0 + s, 0)),
                  pl.BlockSpec((dec_seq, QK_W), lambda s, pt: (blk0 + s, 1)),
                  pl.BlockSpec((dec_seq, ATT_W), lambda s, pt: (blk0 + s, 2)),
                  pl.BlockSpec((4, D_QK), lambda s, pt: (0, 0)),
                  pl.BlockSpec((1, D_VA), lambda s, pt: (0, 0))]
                 + [k_spec(p) for p in range(n_pages)]
                 + [v_spec(p) for p in range(n_pages)],
        out_specs=pl.BlockSpec((dec_seq, ATT_W), lambda s, pt: (s, 0)),
    )
    return pl.pallas_call(
        kern,
        out_shape=jax.ShapeDtypeStruct((dec_b * dec_seq, ATT_W), F32),
        grid_spec=grid_spec,
        compiler_params=_cparams(("arbitrary",)),
        name="attn_sample",
    )(page_table, qkvag, qkvag, qkvag, lam_p, subln_g.reshape(1, D_VA),
      *([cache_kt] * n_pages), *([cache_vr] * n_pages))


def _conv_taps(scr, cw_ref, row0, rows):
    acc = None
    for w in range(CONV_W):
        term = scr[pl.ds(row0 + w, rows), :] * cw_ref[w:w + 1, :]
        acc = term if acc is None else acc + term
    return acc


def _conv_prompt_kernel(a_ref, g_ref, ah_ref, gh_ref, cw_ref, cb_ref, lg_ref, lb_ref,
                        c_ref, st_ref, scr, *, ts, chunk):
    i = pl.program_id(1)
    n = pl.num_programs(1)
    u = a_ref[...] * _sigmoid(g_ref[...])
    uh = ah_ref[...] * _sigmoid(gh_ref[...])
    uh = jnp.where(i > 0, uh, 0.0)
    scr[0:CONV_HALO, :] = uh
    scr[CONV_HALO:CONV_HALO + ts, :] = u
    off = CONV_HALO - (CONV_W - 1)
    for c0 in range(0, ts, chunk):
        c = _conv_taps(scr, cw_ref, c0 + off, chunk) + cb_ref[...]
        c_ref[c0:c0 + chunk, :] = _silu(_layer_norm_rows(c, lg_ref[...], lb_ref[...]))

    @pl.when(i == n - 1)
    def _():
        st_ref[...] = scr[CONV_HALO + ts - (CONV_W - 1):CONV_HALO + ts, :]


def _conv_prompt(qkvag, cw, cb, lg, lb, bsz, seq, ts):
    ns = seq // ts
    hb = ts // CONV_HALO
    acol = (2 * QK_W + ATT_W) // CONV_CH
    kern = functools.partial(_conv_prompt_kernel, ts=ts, chunk=32)
    vec = lambda b, i: (0, 0)
    return pl.pallas_call(
        kern,
        out_shape=(jax.ShapeDtypeStruct((bsz * seq, CONV_CH), F32),
                   jax.ShapeDtypeStruct((bsz, CONV_W - 1, CONV_CH), F32)),
        grid=(bsz, ns),
        in_specs=[pl.BlockSpec((ts, CONV_CH), lambda b, i: (b * ns + i, acol)),
                  pl.BlockSpec((ts, CONV_CH), lambda b, i: (b * ns + i, acol + 1)),
                  pl.BlockSpec((CONV_HALO, CONV_CH), lambda b, i: (jnp.maximum((b * ns + i) * hb - 1, 0), acol)),
                  pl.BlockSpec((CONV_HALO, CONV_CH), lambda b, i: (jnp.maximum((b * ns + i) * hb - 1, 0), acol + 1)),
                  pl.BlockSpec((CONV_W, CONV_CH), vec),
                  pl.BlockSpec((1, CONV_CH), vec),
                  pl.BlockSpec((1, CONV_CH), vec),
                  pl.BlockSpec((1, CONV_CH), vec)],
        out_specs=(pl.BlockSpec((ts, CONV_CH), lambda b, i: (b * ns + i, 0)),
                   pl.BlockSpec((None, CONV_W - 1, CONV_CH), lambda b, i: (b, 0, 0))),
        scratch_shapes=[pltpu.VMEM((CONV_HALO + ts, CONV_CH), F32)],
        compiler_params=_cparams(("parallel", "arbitrary")),
        name="conv_prompt",
    )(qkvag, qkvag, qkvag, qkvag, cw, cb.reshape(1, -1), lg.reshape(1, -1), lb.reshape(1, -1))


def _conv_sample_kernel(a_ref, g_ref, st_ref, cw_ref, cb_ref, lg_ref, lb_ref, c_ref, so_ref, scr, *, nb, dec_seq):
    hist = CONV_W - 1
    u = a_ref[...] * _sigmoid(g_ref[...])
    for s in range(nb):
        scr[0:hist, :] = st_ref[s]
        scr[hist:hist + dec_seq, :] = u[s * dec_seq:(s + 1) * dec_seq, :]
        c = _conv_taps(scr, cw_ref, 0, dec_seq) + cb_ref[...]
        c_ref[s * dec_seq:(s + 1) * dec_seq, :] = _silu(_layer_norm_rows(c, lg_ref[...], lb_ref[...]))
        so_ref[s] = scr[dec_seq:dec_seq + hist, :]


def _conv_sample(qkvag, state, ia, cw, cb, lg, lb, row0, dec_b, dec_seq, nb):
    acol = (2 * QK_W + ATT_W) // CONV_CH
    blk0 = row0 // (nb * dec_seq)
    kern = functools.partial(_conv_sample_kernel, nb=nb, dec_seq=dec_seq)
    vec = lambda i: (0, 0)
    return pl.pallas_call(
        kern,
        out_shape=(jax.ShapeDtypeStruct((dec_b * dec_seq, CONV_CH), F32),
                   jax.ShapeDtypeStruct((dec_b, CONV_W - 1, CONV_CH), F32)),
        grid=(dec_b // nb,),
        in_specs=[pl.BlockSpec((nb * dec_seq, CONV_CH), lambda i: (blk0 + i, acol)),
                  pl.BlockSpec((nb * dec_seq, CONV_CH), lambda i: (blk0 + i, acol + 1)),
                  pl.BlockSpec((None, nb, CONV_W - 1, CONV_CH), lambda i: (ia, i, 0, 0)),
                  pl.BlockSpec((CONV_W, CONV_CH), vec),
                  pl.BlockSpec((1, CONV_CH), vec),
                  pl.BlockSpec((1, CONV_CH), vec),
                  pl.BlockSpec((1, CONV_CH), vec)],
        out_specs=(pl.BlockSpec((nb * dec_seq, CONV_CH), lambda i: (i, 0)),
                   pl.BlockSpec((nb, CONV_W - 1, CONV_CH), lambda i: (i, 0, 0))),
        scratch_shapes=[pltpu.VMEM((CONV_W - 1 + dec_seq + SUBLANES, CONV_CH), F32)],
        compiler_params=_cparams(("arbitrary",)),
        name="conv_sample",
    )(qkvag, qkvag, state, cw, cb.reshape(1, -1), lg.reshape(1, -1), lb.reshape(1, -1))


def _ret_tables(c):
    log_g = jnp.log1p(-jnp.exp2(-5.0 - jnp.arange(H_C, dtype=F32)))
    idx = jnp.arange(c, dtype=F32)
    diff = idx[:, None] - idx[None, :]
    causal = diff >= 0
    d_intra = jnp.where(causal[None], jnp.exp(jnp.where(causal, diff, 0.0)[None] * log_g[:, None, None]), 0.0)
    q_dec = jnp.exp((idx[None, :] + 1.0) * log_g[:, None])[..., None]
    k_dec = jnp.exp((c - 1.0 - idx[None, :]) * log_g[:, None])[..., None]
    c_dec = jnp.exp(c * log_g)[:, None, None]
    return d_intra, q_dec, k_dec, c_dec


def _head_norm_gate(o, g):
    mu = jnp.mean(o, -1, keepdims=True)
    d = o - mu
    var = jnp.mean(d * d, -1, keepdims=True)
    return _silu(g) * (d * lax.rsqrt(var + LN_EPS))


def _ret_step(q, k, v, s, d_intra, q_dec, k_dec, c_dec):
    vb = v.astype(BF16)
    sc = _bdot_nt(q, k) * d_intra
    o = jnp.dot(sc.astype(BF16), vb, preferred_element_type=F32) + _bdot(q * q_dec, s)
    s_new = s * c_dec + jnp.dot((k * k_dec).T.astype(BF16), vb, preferred_element_type=F32)
    return o, s_new


def _ret_prompt_kernel(q_ref, k_ref, v_ref, g_ref, di_ref, qd_ref, kd_ref, cd_ref, o_ref, so_ref, s_scr):
    ci = pl.program_id(1)

    @pl.when(ci == 0)
    def _():
        s_scr[...] = jnp.zeros(s_scr.shape, F32)

    for h in range(H_C):
        o, s_new = _ret_step(q_ref[:, h * DK_C:(h + 1) * DK_C], k_ref[:, h * DK_C:(h + 1) * DK_C],
                             v_ref[:, h * DV_C:(h + 1) * DV_C], s_scr[h],
                             di_ref[h], qd_ref[h], kd_ref[h], cd_ref[h])
        s_scr[h] = s_new
        o_ref[:, h * DV_C:(h + 1) * DV_C] = _head_norm_gate(o, g_ref[:, h * DV_C:(h + 1) * DV_C])

    @pl.when(ci == pl.num_programs(1) - 1)
    def _():
        so_ref[...] = s_scr[...]


def _ret_prompt(qkvg, bsz, seq, chunk):
    nc = seq // chunk
    di, qd, kd, cd = _ret_tables(chunk)
    qw = H_C * DK_C
    vw = H_C * DV_C
    whole3 = lambda b, c: (0, 0, 0)
    return pl.pallas_call(
        _ret_prompt_kernel,
        out_shape=(jax.ShapeDtypeStruct((bsz * seq, vw), F32),
                   jax.ShapeDtypeStruct((bsz, H_C, DK_C, DV_C), F32)),
        grid=(bsz, nc),
        in_specs=[pl.BlockSpec((chunk, qw), lambda b, c: (b * nc + c, 0)),
                  pl.BlockSpec((chunk, qw), lambda b, c: (b * nc + c, 1)),
                  pl.BlockSpec((chunk, vw), lambda b, c: (b * nc + c, 1)),
                  pl.BlockSpec((chunk, vw), lambda b, c: (b * nc + c, 2)),
                  pl.BlockSpec((H_C, chunk, chunk), whole3),
                  pl.BlockSpec((H_C, chunk, 1), whole3),
                  pl.BlockSpec((H_C, chunk, 1), whole3),
                  pl.BlockSpec((H_C, 1, 1), whole3)],
        out_specs=(pl.BlockSpec((chunk, vw), lambda b, c: (b * nc + c, 0)),
                   pl.BlockSpec((None, H_C, DK_C, DV_C), lambda b, c: (b, 0, 0, 0))),
        scratch_shapes=[pltpu.VMEM((H_C, DK_C, DV_C), F32)],
        compiler_params=_cparams(("parallel", "arbitrary")),
        name="ret_prompt",
    )(qkvg, qkvg, qkvg, qkvg, di, qd, kd, cd)


def _ret_sample_kernel(q_ref, k_ref, v_ref, g_ref, s_ref, di_ref, qd_ref, kd_ref, cd_ref, o_ref, so_ref):
    for h in range(H_C):
        o, s_new = _ret_step(q_ref[:, h * DK_C:(h + 1) * DK_C], k_ref[:, h * DK_C:(h + 1) * DK_C],
                             v_ref[:, h * DV_C:(h + 1) * DV_C], s_ref[h],
                             di_ref[h], qd_ref[h], kd_ref[h], cd_ref[h])
        so_ref[h] = s_new
        o_ref[:, h * DV_C:(h + 1) * DV_C] = _head_norm_gate(o, g_ref[:, h * DV_C:(h + 1) * DV_C])


def _ret_sample(qkvg, state, ic, row0, dec_b, dec_seq):
    di, qd, kd, cd = _ret_tables(dec_seq)
    blk0 = row0 // dec_seq
    qw = H_C * DK_C
    vw = H_C * DV_C
    whole3 = lambda s: (0, 0, 0)
    return pl.pallas_call(
        _ret_sample_kernel,
        out_shape=(jax.ShapeDtypeStruct((dec_b * dec_seq, vw), F32),
                   jax.ShapeDtypeStruct((dec_b, H_C, DK_C, DV_C), F32)),
        grid=(dec_b,),
        in_specs=[pl.BlockSpec((dec_seq, qw), lambda s: (blk0 + s, 0)),
                  pl.BlockSpec((dec_seq, qw), lambda s: (blk0 + s, 1)),
                  pl.BlockSpec((dec_seq, vw), lambda s: (blk0 + s, 1)),
                  pl.BlockSpec((dec_seq, vw), lambda s: (blk0 + s, 2)),
                  pl.BlockSpec((None, None, H_C, DK_C, DV_C), lambda s: (ic, s, 0, 0, 0)),
                  pl.BlockSpec((H_C, dec_seq, dec_seq), whole3),
                  pl.BlockSpec((H_C, dec_seq, 1), whole3),
                  pl.BlockSpec((H_C, dec_seq, 1), whole3),
                  pl.BlockSpec((H_C, 1, 1), whole3)],
        out_specs=(pl.BlockSpec((dec_seq, vw), lambda s: (s, 0)),
                   pl.BlockSpec((None, H_C, DK_C, DV_C), lambda s: (s, 0, 0, 0))),
        compiler_params=_cparams(("arbitrary",)),
        name="ret_sample",
    )(qkvg, qkvg, qkvg, qkvg, state, di, qd, kd, cd)


def _route(logits):
    rows = [logits[e:e + 1, :] for e in range(N_EXPERTS)]
    mx = functools.reduce(jnp.maximum, rows)
    ex = [jnp.exp(r - mx) for r in rows]
    den = functools.reduce(lambda a, b: a + b, ex)
    probs = [e / den for e in ex]
    return probs


def _pick_top2(probs, bias_ref):
    sel = [probs[e] + bias_ref[e:e + 1, :] for e in range(N_EXPERTS)]
    epg = EXPERTS_PER_GROUP
    neg = jnp.full_like(sel[0], -jnp.inf)
    gscore = []
    for g in range(N_GROUPS):
        grp = sel[g * epg:(g + 1) * epg]
        pairs = [grp[a] + grp[b] for a in range(epg) for b in range(a + 1, epg)]
        gscore.append(functools.reduce(jnp.maximum, pairs))
    best = functools.reduce(jnp.maximum, gscore)
    g_idx = jnp.full(best.shape, N_GROUPS - 1, jnp.int32)
    for g in range(N_GROUPS - 2, -1, -1):
        g_idx = jnp.where(gscore[g] == best, g, g_idx)
    ing = []
    inp = []
    for j in range(epg):
        sv = sel[j]
        pv = probs[j]
        for g in range(1, N_GROUPS):
            sv = jnp.where(g_idx == g, sel[g * epg + j], sv)
            pv = jnp.where(g_idx == g, probs[g * epg + j], pv)
        ing.append(sv)
        inp.append(pv)
    top1 = functools.reduce(jnp.maximum, ing)
    l1 = jnp.full(best.shape, epg - 1, jnp.int32)
    for j in range(epg - 2, -1, -1):
        l1 = jnp.where(ing[j] == top1, j, l1)
    rest = [jnp.where(l1 == j, neg, ing[j]) for j in range(epg)]
    top2 = functools.reduce(jnp.maximum, rest)
    l2 = jnp.full(best.shape, epg - 1, jnp.int32)
    for j in range(epg - 2, -1, -1):
        l2 = jnp.where(jnp.logical_and(rest[j] == top2, l1 != j), j, l2)
    p1 = functools.reduce(lambda a, b: a + b, [jnp.where(l1 == j, inp[j], 0.0) for j in range(epg)])
    p2 = functools.reduce(lambda a, b: a + b, [jnp.where(l2 == j, inp[j], 0.0) for j in range(epg)])
    tot = p1 + p2
    return g_idx * epg + l1, g_idx * epg + l2, p1 / tot, p2 / tot


def _outproj_kernel(*refs, n_in, n_prompt_tiles):
    p_refs = refs[:n_in]
    s_refs = refs[n_in:2 * n_in]
    x_ref, w_ref, lg_ref, lb_ref, wr_ref, br_ref, o_ref, eidx_ref, ew_ref, a_scr = refs[2 * n_in:]
    i = pl.program_id(0)

    def stage(part_refs):
        col = 0
        for r in part_refs:
            a_scr[:, col:col + r.shape[1]] = r[...].astype(BF16)
            col += r.shape[1]

    @pl.when(i < n_prompt_tiles)
    def _():
        stage(p_refs)

    @pl.when(i >= n_prompt_tiles)
    def _():
        stage(s_refs)

    mix = jnp.dot(a_scr[...], w_ref[...], preferred_element_type=F32)
    x1 = _layer_norm_rows(DN_ALPHA * x_ref[...] + mix, lg_ref[...], lb_ref[...])
    o_ref[...] = x1
    logits = lax.dot_general(wr_ref[...], x1, (((1,), (1,)), ((), ())),
                             precision=lax.Precision.HIGHEST, preferred_element_type=F32)
    probs = _route(logits)
    e1, e2, w1, w2 = _pick_top2(probs, br_ref)
    eidx_ref[0:1, :] = e1
    eidx_ref[1:2, :] = e2
    ew_ref[0:1, :] = w1
    ew_ref[1:2, :] = w2


def _outproj(parts_p, parts_s, x, w, li, lg, lb, w_router_t, b_router, tm, name):
    t_all = x.shape[0]
    n_pt = parts_p[0].shape[0] // tm
    kern = functools.partial(_outproj_kernel, n_in=len(parts_p), n_prompt_tiles=n_pt)
    vec = lambda i: (0, 0)
    return pl.pallas_call(
        kern,
        out_shape=(jax.ShapeDtypeStruct((t_all, D_MODEL), F32),
                   jax.ShapeDtypeStruct((2, t_all), jnp.int32),
                   jax.ShapeDtypeStruct((2, t_all), F32)),
        grid=(t_all // tm,),
        in_specs=[pl.BlockSpec((tm, p.shape[1]), lambda i: (jnp.minimum(i, n_pt - 1), 0)) for p in parts_p]
                 + [pl.BlockSpec((tm, p.shape[1]), lambda i: (jnp.maximum(i - n_pt, 0), 0)) for p in parts_s]
                 + [pl.BlockSpec((tm, D_MODEL), lambda i: (i, 0)),
                    pl.BlockSpec((None,) + w.shape[1:], lambda i: (li, 0, 0)),
                    pl.BlockSpec((1, D_MODEL), vec),
                    pl.BlockSpec((1, D_MODEL), vec),
                    pl.BlockSpec((N_EXPERTS, D_MODEL), vec),
                    pl.BlockSpec((N_EXPERTS, 1), vec)],
        out_specs=(pl.BlockSpec((tm, D_MODEL), lambda i: (i, 0)),
                   pl.BlockSpec((2, tm), lambda i: (0, i)),
                   pl.BlockSpec((2, tm), lambda i: (0, i))),
        scratch_shapes=[pltpu.VMEM((tm, w.shape[1]), BF16)],
        compiler_params=_cparams(("parallel",)),
        name=name,
    )(*parts_p, *parts_s, x, w, lg.reshape(1, -1), lb.reshape(1, -1), w_router_t, b_router.reshape(-1, 1))


def _moe_plan(e_idx, tile):
    n_assign = e_idx.size
    e = e_idx.reshape(n_assign)
    onehot = (e[:, None] == jnp.arange(N_EXPERTS, dtype=jnp.int32)[None, :]).astype(jnp.int32)
    csum = jnp.cumsum(onehot, axis=0)
    counts = csum[-1]
    ends = jnp.cumsum(counts)
    starts = ends - counts
    pos = jnp.sum(onehot * (starts[None, :] + csum - 1), axis=1)
    n_tiles = n_assign // tile
    n_steps = n_tiles + N_EXPERTS - 1
    first_tile = starts // tile
    tiles_e = jnp.where(counts > 0, (ends + tile - 1) // tile - first_tile, 0)
    step_end = jnp.cumsum(tiles_e)
    step_start = step_end - tiles_e
    total = step_end[-1]
    sidx = jnp.arange(n_steps, dtype=jnp.int32)
    valid = sidx < total
    s_eff = jnp.minimum(sidx, total - 1)
    exp_s = jnp.sum((step_end[None, :] <= s_eff[:, None]).astype(jnp.int32), axis=1)
    exp_s = jnp.minimum(exp_s, N_EXPERTS - 1)
    tile_s = first_tile[exp_s] + (s_eff - step_start[exp_s])
    lo = jnp.maximum(starts[exp_s], tile_s * tile) - tile_s * tile
    hi = jnp.minimum(ends[exp_s], (tile_s + 1) * tile) - tile_s * tile
    prev_tile = jnp.concatenate([jnp.full((1,), -1, jnp.int32), tile_s[:-1]])
    first = jnp.logical_and(valid, tile_s != prev_tile)
    i32 = lambda v: v.astype(jnp.int32)
    return i32(pos), (i32(tile_s), exp_s, i32(lo), i32(hi), i32(first), i32(valid))


def _dispatch_kernel(pos_ref, x_ref, xs_hbm, sem, *, tc):
    def row_copy(k, r):
        return pltpu.make_async_copy(x_ref.at[pl.ds(r, 1), :], xs_hbm.at[pl.ds(pos_ref[k, r], 1), :], sem)

    for k in range(2):
        lax.fori_loop(0, tc, lambda r, c: (row_copy(k, r).start(), c)[1], 0, unroll=8)
    for k in range(2):
        lax.fori_loop(0, tc, lambda r, c: (row_copy(k, r).wait(), c)[1], 0, unroll=8)


def _dispatch(x, pos3, tc):
    n_tok, d = x.shape
    kern = functools.partial(_dispatch_kernel, tc=tc)
    return pl.pallas_call(
        kern,
        out_shape=jax.ShapeDtypeStruct((2 * n_tok, d), x.dtype),
        grid=(n_tok // tc,),
        in_specs=[pl.BlockSpec((None, 2, tc), lambda i: (i, 0, 0), memory_space=pltpu.SMEM),
                  pl.BlockSpec((tc, d), lambda i: (i, 0))],
        out_specs=pl.BlockSpec(memory_space=pl.ANY),
        scratch_shapes=[pltpu.SemaphoreType.DMA],
        compiler_params=_cparams(("arbitrary",)),
        name="moe_dispatch",
    )(pos3, x)


def _gmm_kernel(tile_ref, exp_ref, lo_ref, hi_ref, first_ref, valid_ref, xs_ref, wg_ref, wu_ref, wd_ref,
                o_ref, wg_b, wu_b, wd_b):
    s = pl.program_id(0)
    prev = jnp.maximum(s - 1, 0)
    new_expert = jnp.logical_or(s == 0, exp_ref[s] != exp_ref[prev])

    @pl.when(new_expert)
    def _():
        wg_b[...] = wg_ref[...].astype(BF16)
        wu_b[...] = wu_ref[...].astype(BF16)
        wd_b[...] = wd_ref[...].astype(BF16)

    @pl.when(valid_ref[s] == 1)
    def _():
        xb = xs_ref[...].astype(BF16)
        hid = _silu(jnp.dot(xb, wg_b[...], preferred_element_type=F32)) * \
            jnp.dot(xb, wu_b[...], preferred_element_type=F32)
        y = jnp.dot(hid.astype(BF16), wd_b[...], preferred_element_type=F32)
        rows = lax.broadcasted_iota(jnp.int32, (y.shape[0], 1), 0)
        mine = jnp.logical_and(rows >= lo_ref[s], rows < hi_ref[s])

        @pl.when(first_ref[s] == 1)
        def _():
            o_ref[...] = jnp.where(mine, y, 0.0)

        @pl.when(first_ref[s] == 0)
        def _():
            o_ref[...] = jnp.where(mine, y, o_ref[...])


def _gmm(xs, plan, wg, wu, wd, layer, tile):
    n_rows, d = xs.shape
    n_steps = plan[0].shape[0]
    grid_spec = pltpu.PrefetchScalarGridSpec(
        num_scalar_prefetch=len(plan),
        grid=(n_steps,),
        in_specs=[pl.BlockSpec((tile, d), lambda s, t, e, *_: (t[s], 0)),
                  pl.BlockSpec((None, None, d, D_FF), lambda s, t, e, *_: (layer, e[s], 0, 0)),
                  pl.BlockSpec((None, None, d, D_FF), lambda s, t, e, *_: (layer, e[s], 0, 0)),
                  pl.BlockSpec((None, None, D_FF, d), lambda s, t, e, *_: (layer, e[s], 0, 0))],
        out_specs=pl.BlockSpec((tile, d), lambda s, t, e, *_: (t[s], 0)),
        scratch_shapes=[pltpu.VMEM((d, D_FF), BF16), pltpu.VMEM((d, D_FF), BF16), pltpu.VMEM((D_FF, d), BF16)],
    )
    return pl.pallas_call(
        _gmm_kernel,
        out_shape=jax.ShapeDtypeStruct((n_rows, d), F32),
        grid_spec=grid_spec,
        compiler_params=_cparams(("arbitrary",)),
        name="moe_gmm",
    )(*plan, xs, wg, wu, wd)


def _combine_kernel(pos_ref, posn_ref, w_ref, x_ref, lg_ref, lb_ref, ys_hbm, o_ref, buf, sem, *, tc):
    i = pl.program_id(0)
    n = pl.num_programs(0)
    slot = i % 2

    def row_copy(p_ref, k, r, sl):
        return pltpu.make_async_copy(ys_hbm.at[pl.ds(p_ref[k, r], 1), :], buf.at[sl, k, pl.ds(r, 1), :],
                                     sem.at[sl])

    def start_tile(p_ref, sl):
        for k in range(2):
            lax.fori_loop(0, tc, lambda r, c: (row_copy(p_ref, k, r, sl).start(), c)[1], 0, unroll=8)

    @pl.when(i == 0)
    def _():
        start_tile(pos_ref, 0)

    @pl.when(i + 1 < n)
    def _():
        start_tile(posn_ref, 1 - slot)

    for k in range(2):
        lax.fori_loop(0, tc, lambda r, c: (row_copy(pos_ref, k, r, slot).wait(), c)[1], 0, unroll=8)

    w = w_ref[...]
    sub = lax.broadcasted_iota(jnp.int32, (LANES, LANES), 0)
    for c0 in range(0, tc, LANES):
        wsq = jnp.where(sub == 0, w[0:1, c0:c0 + LANES], jnp.where(sub == 1, w[1:2, c0:c0 + LANES], 0.0))
        wt = wsq.T
        y = wt[:, 0:1] * buf[slot, 0, c0:c0 + LANES, :] + wt[:, 1:2] * buf[slot, 1, c0:c0 + LANES, :]
        o_ref[c0:c0 + LANES, :] = _layer_norm_rows(DN_ALPHA * x_ref[c0:c0 + LANES, :] + y,
                                                   lg_ref[...], lb_ref[...])


def _combine(ys, pos3, ew, x, lg, lb, tc):
    n_tok, d = x.shape
    n_t = n_tok // tc
    kern = functools.partial(_combine_kernel, tc=tc)
    vec = lambda i: (0, 0)
    return pl.pallas_call(
        kern,
        out_shape=jax.ShapeDtypeStruct((n_tok, d), F32),
        grid=(n_t,),
        in_specs=[pl.BlockSpec((None, 2, tc), lambda i: (i, 0, 0), memory_space=pltpu.SMEM),
                  pl.BlockSpec((None, 2, tc), lambda i: (jnp.minimum(i + 1, n_t - 1), 0, 0),
                               memory_space=pltpu.SMEM),
                  pl.BlockSpec((2, tc), lambda i: (0, i)),
                  pl.BlockSpec((tc, d), lambda i: (i, 0)),
                  pl.BlockSpec((1, d), vec),
                  pl.BlockSpec((1, d), vec),
                  pl.BlockSpec(memory_space=pl.ANY)],
        out_specs=pl.BlockSpec((tc, d), lambda i: (i, 0)),
        scratch_shapes=[pltpu.VMEM((2, 2, tc, d), F32), pltpu.SemaphoreType.DMA((2,))],
        compiler_params=_cparams(("arbitrary",)),
        name="moe_combine",
    )(pos3, pos3, ew, x, lg.reshape(1, -1), lb.reshape(1, -1), ys)


def _moe(x, e_idx, ew, wg, wu, wd, layer, lg, lb):
    pos, plan = _moe_plan(e_idx, MOE_TILE)
    n_t = x.shape[0] // MOE_TILE
    pos3 = pos.reshape(2, n_t, MOE_TILE).transpose(1, 0, 2)
    xs = _dispatch(x, pos3, MOE_TILE)
    ys = _gmm(xs, plan, wg, wu, wd, layer, MOE_TILE)
    return _combine(ys, pos3, ew, x, lg, lb, MOE_TILE)


def kernel(x_prompt, x_sample, cache_k, cache_v, page_table, state_conv, state_ret, w_in_a, lambda_a, subln_a,
           conv_w, conv_b, conv_ln_g, conv_ln_b, w_out_a, w_in_c, w_out_c, ln_g, ln_b, w_router, b_router,
           w_e_gate, w_e_up, w_e_down):
    bsz, seq, _ = x_prompt.shape
    dec_b, dec_seq, _ = x_sample.shape
    n_pages = page_table.shape[1]
    past_len = n_pages * PAGE_SIZE
    tp = bsz * seq
    ts_ = dec_b * dec_seq
    tm = min(512, seq)
    tm_in = min(1024, seq)
    assert seq % tm_in == 0 and ts_ % tm_in == 0 and dec_seq == SUBLANES
    seq_tiles = seq // tm_in
    n_prompt_tiles = tp // tm_in

    pos_p = jnp.arange(seq, dtype=jnp.int32)
    pos_s = past_len + (jnp.arange(tm_in, dtype=jnp.int32) % dec_seq)
    pos_tab = jnp.concatenate([pos_p, pos_s])
    tabs_a = _rot_tables_a(pos_tab)
    tabs_c = _rot_tables_c(pos_tab)

    n_phys = cache_k.shape[1]
    ck = jnp.transpose(cache_k, (0, 1, 3, 4, 2))
    cv = cache_v.reshape(cache_v.shape[0], n_phys, PAGE_SIZE * H_A, D_VA)
    w_router_t = w_router.T
    w_in_a, w_out_a, w_in_c, w_out_c = (w.astype(BF16) for w in (w_in_a, w_out_a, w_in_c, w_out_c))

    x = jnp.concatenate([x_prompt.reshape(tp, D_MODEL), x_sample.reshape(ts_, D_MODEL)], axis=0)
    k_p, v_p, conv_p, ret_p, k_s, v_s, conv_s, ret_s = [], [], [], [], [], [], [], []
    for l in range(DEPTH):
        if l % 2 == 0:
            ia = l // 2
            lam_init = 0.8 - 0.6 * math.exp(-0.3 * l)
            qkvag = _inproj(x, w_in_a, ia, tabs_a, _inproj_a_kernel, tm_in, seq_tiles, n_prompt_tiles, "inproj_a")
            att_p = _attn_prompt(qkvag, lambda_a[ia], subln_a[ia], bsz, seq, lam_init, min(256, seq))
            att_s = _attn_sample(qkvag, ck, cv, ia, page_table, lambda_a[ia], subln_a[ia], tp, dec_b,
                                 dec_seq, lam_init)
            c_p, st_p = _conv_prompt(qkvag, conv_w[ia], conv_b[ia], conv_ln_g[ia], conv_ln_b[ia], bsz, seq,
                                     min(256, seq))
            c_s, st_s = _conv_sample(qkvag, state_conv, ia, conv_w[ia], conv_b[ia], conv_ln_g[ia],
                                     conv_ln_b[ia], tp, dec_b, dec_seq, 8)
            x, e_idx, ew = _outproj([att_p, c_p], [att_s, c_s], x, w_out_a, ia, ln_g[l, 0], ln_b[l, 0], w_router_t,
                                    b_router, tm, "outproj_a")
            k_all = qkvag[:, QK_W:2 * QK_W]
            v_all = qkvag[:, 2 * QK_W:2 * QK_W + ATT_W]
            k_p.append(k_all[:tp].reshape(bsz, seq, 2 * H_A, D_QK))
            v_p.append(v_all[:tp].reshape(bsz, seq, H_A, D_VA))
            k_s.append(k_all[tp:].reshape(dec_b, dec_seq, 2 * H_A, D_QK))
            v_s.append(v_all[tp:].reshape(dec_b, dec_seq, H_A, D_VA))
            conv_p.append(st_p)
            conv_s.append(st_s)
        else:
            ic = l // 2
            qkvg = _inproj(x, w_in_c, ic, tabs_c, _inproj_c_kernel, tm_in, seq_tiles, n_prompt_tiles, "inproj_c")
            og_p, s_p = _ret_prompt(qkvg, bsz, seq, min(256, seq))
            og_s, s_s = _ret_sample(qkvg, state_ret, ic, tp, dec_b, dec_seq)
            x, e_idx, ew = _outproj([og_p], [og_s], x, w_out_c, ic, ln_g[l, 0], ln_b[l, 0], w_router_t, b_router,
                                    tm, "outproj_c")
            ret_p.append(s_p)
            ret_s.append(s_s)
        x = _moe(x, e_idx, ew, w_e_gate, w_e_up, w_e_down, l, ln_g[l, 1], ln_b[l, 1])

    y_prompt = x[:tp].reshape(bsz, seq, D_MODEL)
    y_sample = x[tp:].reshape(dec_b, dec_seq, D_MODEL)
    return (y_prompt, y_sample, jnp.stack(k_p), jnp.stack(v_p), jnp.stack(conv_p), jnp.stack(ret_p),
            jnp.stack(k_s), jnp.stack(v_s), jnp.stack(conv_s), jnp.stack(ret_s))
```

```python
import functools
import math

import jax
import jax.numpy as jnp
from jax import lax
from jax.experimental import pallas as pl
from jax.experimental.pallas import tpu as pltpu

F32 = jnp.float32
BF16 = jnp.bfloat16

D_MODEL = 1024
DEPTH = 4
PAGE_SIZE = 128
H_A = 4
D_QK = 64
D_VA = 2 * D_QK
QK_W = H_A * 2 * D_QK
ATT_W = H_A * D_VA
ROT_DIM = D_QK // 4
ROPE_THETA = 500000.0
CONV_CH = D_MODEL // 2
CONV_W = 31
H_C = 4
DK_C = D_MODEL // H_C
DV_C = 2 * DK_C
RET_THETA = 10000.0
N_EXPERTS = 16
N_GROUPS = 4
EXPERTS_PER_GROUP = N_EXPERTS // N_GROUPS
D_FF = D_MODEL // 2
DN_ALPHA = (2 * DEPTH) ** 0.25
LN_EPS = 1e-5
IN_A = 2 * QK_W + ATT_W + 2 * CONV_CH
IN_C = 2 * H_C * DK_C + 2 * H_C * DV_C

LANES = 128
SUBLANES = 8
CONV_HALO = 32
VMEM_LIMIT = 56 * 1024 * 1024
MOE_TILE = 256


def _cparams(sem):
    return pltpu.CompilerParams(dimension_semantics=sem, vmem_limit_bytes=VMEM_LIMIT)


def _bdot(a, b):
    return jnp.dot(a.astype(BF16), b.astype(BF16), preferred_element_type=F32)


def _bdot_nt(a, b):
    return lax.dot_general(a.astype(BF16), b.astype(BF16), (((1,), (1,)), ((), ())),
                           preferred_element_type=F32)


def _layer_norm_rows(v, g, b):
    mu = jnp.mean(v, -1, keepdims=True)
    d = v - mu
    var = jnp.mean(d * d, -1, keepdims=True)
    return d * lax.rsqrt(var + LN_EPS) * g + b


def _silu(v):
    return v * (1.0 / (1.0 + jnp.exp(-v)))


def _sigmoid(v):
    return 1.0 / (1.0 + jnp.exp(-v))


def _cast_rows_once(x_ref, xb_ref):
    @pl.when(pl.program_id(1) == 0)
    def _():
        xb_ref[...] = x_ref[...].astype(BF16)


def _inproj_a_kernel(x_ref, w_ref, cos_ref, sa_ref, sb_ref, o_ref, xb_ref):
    j = pl.program_id(1)
    _cast_rows_once(x_ref, xb_ref)
    y = jnp.dot(xb_ref[...], w_ref[...], preferred_element_type=F32)

    @pl.when(j < 2)
    def _():
        cos = cos_ref[...]
        sa = sa_ref[...]
        sb = sb_ref[...]
        for blk in range(QK_W // LANES):
            t = y[:, blk * LANES:(blk + 1) * LANES]
            r = (t * cos + pltpu.roll(t, ROT_DIM // 2, 1) * sa
                 + pltpu.roll(t, LANES - ROT_DIM // 2, 1) * sb)
            o_ref[:, blk * LANES:(blk + 1) * LANES] = r

    @pl.when(j >= 2)
    def _():
        o_ref[...] = y


def _rot_tables_a(pos):
    half = ROT_DIM // 2
    inv = ROPE_THETA ** (-jnp.arange(0, ROT_DIM, 2, dtype=F32) / ROT_DIM)
    ang = pos.astype(F32)[:, None] * inv[None, :]
    c, s = jnp.cos(ang), jnp.sin(ang)
    n = pos.shape[0]
    one = jnp.ones((n, D_QK - ROT_DIM), F32)
    zero = jnp.zeros((n, D_QK - ROT_DIM), F32)
    zh = jnp.zeros((n, half), F32)
    cos64 = jnp.concatenate([c, c, one], 1)
    sa64 = jnp.concatenate([zh, s, zero], 1)
    sb64 = jnp.concatenate([-s, zh, zero], 1)
    rep = LANES // D_QK
    return jnp.tile(cos64, (1, rep)), jnp.tile(sa64, (1, rep)), jnp.tile(sb64, (1, rep))


def _inproj_c_kernel(x_ref, w_ref, cos_ref, sin_ref, o_ref, xb_ref):
    j = pl.program_id(1)
    _cast_rows_once(x_ref, xb_ref)
    y = jnp.dot(xb_ref[...], w_ref[...], preferred_element_type=F32)

    @pl.when(j < 4)
    def _():
        cos = cos_ref[...]
        sin = sin_ref[...]
        scale = jnp.where(j >= 2, DK_C ** -0.5, 1.0).astype(F32)
        half = DK_C // 2
        for hd in range(2):
            t1 = y[:, hd * DK_C:hd * DK_C + half]
            t2 = y[:, hd * DK_C + half:(hd + 1) * DK_C]
            o_ref[:, hd * DK_C:hd * DK_C + half] = (t1 * cos - t2 * sin) * scale
            o_ref[:, hd * DK_C + half:(hd + 1) * DK_C] = (t2 * cos + t1 * sin) * scale

    @pl.when(j >= 4)
    def _():
        o_ref[...] = y


def _rot_tables_c(pos):
    inv = RET_THETA ** (-jnp.linspace(0.0, 1.0, DK_C // 2, dtype=F32))
    ang = pos.astype(F32)[:, None] * inv[None, :]
    return jnp.cos(ang), jnp.sin(ang)


def _inproj(x, w, li, tables, kern, tm, seq_tiles, n_prompt_tiles, name):
    t_all, d = x.shape
    n = w.shape[2]
    tn = 512
    tw = tables[0].shape[1]

    def tab_map(i, j):
        return (jnp.where(i < n_prompt_tiles, i % seq_tiles, seq_tiles), 0)

    return pl.pallas_call(
        kern,
        out_shape=jax.ShapeDtypeStruct((t_all, n), F32),
        grid=(t_all // tm, n // tn),
        in_specs=[pl.BlockSpec((tm, d), lambda i, j: (i, 0)),
                  pl.BlockSpec((None, d, tn), lambda i, j: (li, 0, j))]
                 + [pl.BlockSpec((tm, tw), tab_map) for _ in tables],
        out_specs=pl.BlockSpec((tm, tn), lambda i, j: (i, j)),
        scratch_shapes=[pltpu.VMEM((tm, d), BF16)],
        compiler_params=_cparams(("parallel", "arbitrary")),
        name=name,
    )(x, w, *tables)


def _lambda_value(lam_ref, lam_init):
    lf = lam_ref[...]
    a = jnp.sum(lf[0:1, :] * lf[1:2, :], axis=-1, keepdims=True)
    b = jnp.sum(lf[2:3, :] * lf[3:4, :], axis=-1, keepdims=True)
    return jnp.exp(a) - jnp.exp(b) + lam_init


def _sub_norm(o, g, lam_init):
    ms = jnp.mean(o * o, -1, keepdims=True)
    return o * lax.rsqrt(ms + LN_EPS) * g * (1.0 - lam_init)


def _attn_prompt_kernel(q_ref, k_ref, v_ref, lam_ref, g_ref, o_ref, qt_scr, kb_scr, vt_scr, m_scr, acc_scr,
                        *, lam_init, tq):
    qi = pl.program_id(1)
    n_chain = 2 * H_A
    n_kt = k_ref.shape[0] // tq

    @pl.when(qi == 0)
    def _():
        kb_scr[...] = k_ref[...].astype(BF16)
        ones = jnp.ones((D_VA, tq), BF16)
        for j in range(n_kt):
            for h in range(H_A):
                vt_scr[j, h, 0:D_VA, :] = v_ref[j * tq:(j + 1) * tq, h * D_VA:(h + 1) * D_VA].T.astype(BF16)
                vt_scr[j, h, D_VA:2 * D_VA, :] = ones

    sub = lax.broadcasted_iota(jnp.int32, (D_VA, tq), 0)
    for h in range(H_A):
        qht = (q_ref[:, h * D_VA:(h + 1) * D_VA] * (D_QK ** -0.5)).T
        for m in range(2):
            qt_scr[2 * h + m] = jnp.where(sub // D_QK == m, qht, 0.0).astype(BF16)
    m_scr[...] = jnp.full(m_scr.shape, -jnp.inf, F32)
    acc_scr[...] = jnp.zeros(acc_scr.shape, F32)
    key = lax.broadcasted_iota(jnp.int32, (tq, tq), 0)
    qry = lax.broadcasted_iota(jnp.int32, (tq, tq), 1)

    def block(j, diagonal):
        start = pl.multiple_of(j * tq, tq)
        for c in range(n_chain):
            h = c // 2
            st = jnp.dot(kb_scr[pl.ds(start, tq), h * D_VA:(h + 1) * D_VA], qt_scr[c],
                         preferred_element_type=F32)
            if diagonal:
                st = jnp.where(key <= qry, st, -jnp.inf)
            m_prev = m_scr[c]
            m_new = jnp.maximum(m_prev, jnp.max(st, 0, keepdims=True))
            pt = jnp.exp(st - m_new).astype(BF16)
            pv = jnp.dot(vt_scr[j, h], pt, preferred_element_type=F32)
            acc_scr[c] = jnp.exp(m_prev - m_new) * acc_scr[c] + pv
            m_scr[c] = m_new

    def trip(j, carry):
        block(j, False)
        return carry

    lax.fori_loop(0, qi, trip, 0)
    block(qi, True)
    lam = _lambda_value(lam_ref, lam_init)
    for h in range(H_A):
        a0 = acc_scr[2 * h]
        a1 = acc_scr[2 * h + 1]
        ot = a0[:D_VA] / a0[D_VA:D_VA + 1] - lam * (a1[:D_VA] / a1[D_VA:D_VA + 1])
        o_ref[:, h * D_VA:(h + 1) * D_VA] = _sub_norm(ot.T, g_ref[...], lam_init)


def _attn_prompt(qkvag, lam_p, subln_g, bsz, seq, lam_init, tq):
    nq = seq // tq
    kern = functools.partial(_attn_prompt_kernel, lam_init=lam_init, tq=tq)
    return pl.pallas_call(
        kern,
        out_shape=jax.ShapeDtypeStruct((bsz * seq, ATT_W), F32),
        grid=(bsz, nq),
        in_specs=[pl.BlockSpec((tq, QK_W), lambda b, qi: (b * nq + qi, 0)),
                  pl.BlockSpec((seq, QK_W), lambda b, qi: (b, 1)),
                  pl.BlockSpec((seq, ATT_W), lambda b, qi: (b, 2)),
                  pl.BlockSpec((4, D_QK), lambda b, qi: (0, 0)),
                  pl.BlockSpec((1, D_VA), lambda b, qi: (0, 0))],
        out_specs=pl.BlockSpec((tq, ATT_W), lambda b, qi: (b * nq + qi, 0)),
        scratch_shapes=[pltpu.VMEM((2 * H_A, D_VA, tq), BF16),
                        pltpu.VMEM((seq, QK_W), BF16),
                        pltpu.VMEM((nq, H_A, 2 * D_VA, tq), BF16),
                        pltpu.VMEM((2 * H_A, 1, tq), F32),
                        pltpu.VMEM((2 * H_A, 2 * D_VA, tq), F32)],
        compiler_params=_cparams(("parallel", "arbitrary")),
        name="attn_prompt",
    )(qkvag, qkvag, qkvag, lam_p, subln_g.reshape(1, D_VA))


def _attn_sample_kernel(pt_ref, q_ref, kn_ref, vn_ref, lam_ref, g_ref, *rest, lam_init, n_pages, dec_seq):
    k_refs = rest[:n_pages]
    v_refs = rest[n_pages:2 * n_pages]
    o_ref = rest[2 * n_pages]
    nrow = 2 * H_A * dec_seq
    q = q_ref[...] * (D_QK ** -0.5)
    qt = jnp.concatenate([q] * (2 * H_A), axis=0)
    rid = lax.broadcasted_iota(jnp.int32, (nrow, QK_W), 0)
    cid = lax.broadcasted_iota(jnp.int32, (nrow, QK_W), 1)
    qbd = jnp.where(cid // D_QK == rid // dec_seq, qt, 0.0).astype(BF16)

    s_past = [_bdot(qbd, kr[...].reshape(QK_W, PAGE_SIZE)) for kr in k_refs]
    s_new = _bdot_nt(qbd, kn_ref[...])
    qpos = lax.broadcasted_iota(jnp.int32, (nrow, dec_seq), 0) % dec_seq
    kpos = lax.broadcasted_iota(jnp.int32, (nrow, dec_seq), 1)
    s_new = jnp.where(kpos <= qpos, s_new, -jnp.inf)
    m = jnp.max(s_new, -1, keepdims=True)
    for s in s_past:
        m = jnp.maximum(m, jnp.max(s, -1, keepdims=True))
    p_new = jnp.exp(s_new - m)
    l = jnp.sum(p_new, -1, keepdims=True)
    p_past = []
    for s in s_past:
        p = jnp.exp(s - m)
        l = l + jnp.sum(p, -1, keepdims=True)
        p_past.append(p.astype(BF16))
    lam = _lambda_value(lam_ref, lam_init)
    g = g_ref[...]
    grp = 2 * dec_seq
    for h in range(H_A):
        r0 = h * grp
        acc = _bdot(p_new[r0:r0 + grp], vn_ref[:, h * D_VA:(h + 1) * D_VA])
        for p, vr in zip(p_past, v_refs):
            acc = acc + _bdot(p[r0:r0 + grp], vr[pl.ds(h, PAGE_SIZE, stride=H_A), :])
        acc = acc / l[r0:r0 + grp]
        o = acc[:dec_seq] - lam * acc[dec_seq:]
        o_ref[:, h * D_VA:(h + 1) * D_VA] = _sub_norm(o, g, lam_init)


def _attn_sample(qkvag, cache_kt, cache_vr, ia, page_table, lam_p, subln_g, row0, dec_b, dec_seq, lam_init):
    n_pages = page_table.shape[1]
    kern = functools.partial(_attn_sample_kernel, lam_init=lam_init, n_pages=n_pages, dec_seq=dec_seq)
    blk0 = row0 // dec_seq

    def k_spec(p):
        return pl.BlockSpec((None, None, 2 * H_A, D_QK, PAGE_SIZE), lambda s, pt: (ia, pt[s, p], 0, 0, 0))

    def v_spec(p):
        return pl.BlockSpec((None, None, PAGE_SIZE * H_A, D_VA), lambda s, pt: (ia, pt[s, p], 0, 0))

    grid_spec = pltpu.PrefetchScalarGridSpec(
        num_scalar_prefetch=1,
        grid=(dec_b,),
        in_specs=[pl.BlockSpec((dec_seq, QK_W), lambda s, pt: (blk0 + s, 0)),
                  pl.BlockSpec((dec_seq, QK_W), lambda s, pt: (blk0 + s, 1)),
                  pl.BlockSpec((dec_seq, ATT_W), lambda s, pt: (blk0 + s, 2)),
                  pl.BlockSpec((4, D_QK), lambda s, pt: (0, 0)),
                  pl.BlockSpec((1, D_VA), lambda s, pt: (0, 0))]
                 + [k_spec(p) for p in range(n_pages)]
                 + [v_spec(p) for p in range(n_pages)],
        out_specs=pl.BlockSpec((dec_seq, ATT_W), lambda s, pt: (s, 0)),
    )
    return pl.pallas_call(
        kern,
        out_shape=jax.ShapeDtypeStruct((dec_b * dec_seq, ATT_W), F32),
        grid_spec=grid_spec,
        compiler_params=_cparams(("arbitrary",)),
        name="attn_sample",
    )(page_table, qkvag, qkvag, qkvag, lam_p, subln_g.reshape(1, D_VA),
      *([cache_kt] * n_pages), *([cache_vr] * n_pages))


def _conv_taps(scr, cw_ref, row0, rows):
    acc = None
    for w in range(CONV_W):
        term = scr[pl.ds(row0 + w, rows), :] * cw_ref[w:w + 1, :]
        acc = term if acc is None else acc + term
    return acc


def _conv_prompt_kernel(a_ref, g_ref, ah_ref, gh_ref, cw_ref, cb_ref, lg_ref, lb_ref,
                        c_ref, st_ref, scr, *, ts, chunk):
    i = pl.program_id(1)
    n = pl.num_programs(1)
    u = a_ref[...] * _sigmoid(g_ref[...])
    uh = ah_ref[...] * _sigmoid(gh_ref[...])
    uh = jnp.where(i > 0, uh, 0.0)
    scr[0:CONV_HALO, :] = uh
    scr[CONV_HALO:CONV_HALO + ts, :] = u
    off = CONV_HALO - (CONV_W - 1)
    for c0 in range(0, ts, chunk):
        c = _conv_taps(scr, cw_ref, c0 + off, chunk) + cb_ref[...]
        c_ref[c0:c0 + chunk, :] = _silu(_layer_norm_rows(c, lg_ref[...], lb_ref[...]))

    @pl.when(i == n - 1)
    def _():
        st_ref[...] = scr[CONV_HALO + ts - (CONV_W - 1):CONV_HALO + ts, :]


def _conv_prompt(qkvag, cw, cb, lg, lb, bsz, seq, ts):
    ns = seq // ts
    hb = ts // CONV_HALO
    acol = (2 * QK_W + ATT_W) // CONV_CH
    kern = functools.partial(_conv_prompt_kernel, ts=ts, chunk=32)
    vec = lambda b, i: (0, 0)
    return pl.pallas_call(
        kern,
        out_shape=(jax.ShapeDtypeStruct((bsz * seq, CONV_CH), F32),
                   jax.ShapeDtypeStruct((bsz, CONV_W - 1, CONV_CH), F32)),
        grid=(bsz, ns),
        in_specs=[pl.BlockSpec((ts, CONV_CH), lambda b, i: (b * ns + i, acol)),
                  pl.BlockSpec((ts, CONV_CH), lambda b, i: (b * ns + i, acol + 1)),
                  pl.BlockSpec((CONV_HALO, CONV_CH), lambda b, i: (jnp.maximum((b * ns + i) * hb - 1, 0), acol)),
                  pl.BlockSpec((CONV_HALO, CONV_CH), lambda b, i: (jnp.maximum((b * ns + i) * hb - 1, 0), acol + 1)),
                  pl.BlockSpec((CONV_W, CONV_CH), vec),
                  pl.BlockSpec((1, CONV_CH), vec),
                  pl.BlockSpec((1, CONV_CH), vec),
                  pl.BlockSpec((1, CONV_CH), vec)],
        out_specs=(pl.BlockSpec((ts, CONV_CH), lambda b, i: (b * ns + i, 0)),
                   pl.BlockSpec((None, CONV_W - 1, CONV_CH), lambda b, i: (b, 0, 0))),
        scratch_shapes=[pltpu.VMEM((CONV_HALO + ts, CONV_CH), F32)],
        compiler_params=_cparams(("parallel", "arbitrary")),
        name="conv_prompt",
    )(qkvag, qkvag, qkvag, qkvag, cw, cb.reshape(1, -1), lg.reshape(1, -1), lb.reshape(1, -1))


def _conv_sample_kernel(a_ref, g_ref, st_ref, cw_ref, cb_ref, lg_ref, lb_ref, c_ref, so_ref, scr, *, nb, dec_seq):
    hist = CONV_W - 1
    u = a_ref[...] * _sigmoid(g_ref[...])
    for s in range(nb):
        scr[0:hist, :] = st_ref[s]
        scr[hist:hist + dec_seq, :] = u[s * dec_seq:(s + 1) * dec_seq, :]
        c = _conv_taps(scr, cw_ref, 0, dec_seq) + cb_ref[...]
        c_ref[s * dec_seq:(s + 1) * dec_seq, :] = _silu(_layer_norm_rows(c, lg_ref[...], lb_ref[...]))
        so_ref[s] = scr[dec_seq:dec_seq + hist, :]


def _conv_sample(qkvag, state, ia, cw, cb, lg, lb, row0, dec_b, dec_seq, nb):
    acol = (2 * QK_W + ATT_W) // CONV_CH
    blk0 = row0 // (nb * dec_seq)
    kern = functools.partial(_conv_sample_kernel, nb=nb, dec_seq=dec_seq)
    vec = lambda i: (0, 0)
    return pl.pallas_call(
        kern,
        out_shape=(jax.ShapeDtypeStruct((dec_b * dec_seq, CONV_CH), F32),
                   jax.ShapeDtypeStruct((dec_b, CONV_W - 1, CONV_CH), F32)),
        grid=(dec_b // nb,),
        in_specs=[pl.BlockSpec((nb * dec_seq, CONV_CH), lambda i: (blk0 + i, acol)),
                  pl.BlockSpec((nb * dec_seq, CONV_CH), lambda i: (blk0 + i, acol + 1)),
                  pl.BlockSpec((None, nb, CONV_W - 1, CONV_CH), lambda i: (ia, i, 0, 0)),
                  pl.BlockSpec((CONV_W, CONV_CH), vec),
                  pl.BlockSpec((1, CONV_CH), vec),
                  pl.BlockSpec((1, CONV_CH), vec),
                  pl.BlockSpec((1, CONV_CH), vec)],
        out_specs=(pl.BlockSpec((nb * dec_seq, CONV_CH), lambda i: (i, 0)),
                   pl.BlockSpec((nb, CONV_W - 1, CONV_CH), lambda i: (i, 0, 0))),
        scratch_shapes=[pltpu.VMEM((CONV_W - 1 + dec_seq + SUBLANES, CONV_CH), F32)],
        compiler_params=_cparams(("arbitrary",)),
        name="conv_sample",
    )(qkvag, qkvag, state, cw, cb.reshape(1, -1), lg.reshape(1, -1), lb.reshape(1, -1))


def _ret_tables(c):
    log_g = jnp.log1p(-jnp.exp2(-5.0 - jnp.arange(H_C, dtype=F32)))
    idx = jnp.arange(c, dtype=F32)
    diff = idx[:, None] - idx[None, :]
    causal = diff >= 0
    d_intra = jnp.where(causal[None], jnp.exp(jnp.where(causal, diff, 0.0)[None] * log_g[:, None, None]), 0.0)
    q_dec = jnp.exp((idx[None, :] + 1.0) * log_g[:, None])[..., None]
    k_dec = jnp.exp((c - 1.0 - idx[None, :]) * log_g[:, None])[..., None]
    c_dec = jnp.exp(c * log_g)[:, None, None]
    return d_intra, q_dec, k_dec, c_dec


def _head_norm_gate(o, g):
    mu = jnp.mean(o, -1, keepdims=True)
    d = o - mu
    var = jnp.mean(d * d, -1, keepdims=True)
    return _silu(g) * (d * lax.rsqrt(var + LN_EPS))


def _ret_step(q, k, v, s, d_intra, q_dec, k_dec, c_dec):
    vb = v.astype(BF16)
    sc = _bdot_nt(q, k) * d_intra
    o = jnp.dot(sc.astype(BF16), vb, preferred_element_type=F32) + _bdot(q * q_dec, s)
    s_new = s * c_dec + jnp.dot((k * k_dec).T.astype(BF16), vb, preferred_element_type=F32)
    return o, s_new


def _ret_prompt_kernel(q_ref, k_ref, v_ref, g_ref, di_ref, qd_ref, kd_ref, cd_ref, o_ref, so_ref, s_scr):
    ci = pl.program_id(1)

    @pl.when(ci == 0)
    def _():
        s_scr[...] = jnp.zeros(s_scr.shape, F32)

    for h in range(H_C):
        o, s_new = _ret_step(q_ref[:, h * DK_C:(h + 1) * DK_C], k_ref[:, h * DK_C:(h + 1) * DK_C],
                             v_ref[:, h * DV_C:(h + 1) * DV_C], s_scr[h],
                             di_ref[h], qd_ref[h], kd_ref[h], cd_ref[h])
        s_scr[h] = s_new
        o_ref[:, h * DV_C:(h + 1) * DV_C] = _head_norm_gate(o, g_ref[:, h * DV_C:(h + 1) * DV_C])

    @pl.when(ci == pl.num_programs(1) - 1)
    def _():
        so_ref[...] = s_scr[...]


def _ret_prompt(qkvg, bsz, seq, chunk):
    nc = seq // chunk
    di, qd, kd, cd = _ret_tables(chunk)
    qw = H_C * DK_C
    vw = H_C * DV_C
    whole3 = lambda b, c: (0, 0, 0)
    return pl.pallas_call(
        _ret_prompt_kernel,
        out_shape=(jax.ShapeDtypeStruct((bsz * seq, vw), F32),
                   jax.ShapeDtypeStruct((bsz, H_C, DK_C, DV_C), F32)),
        grid=(bsz, nc),
        in_specs=[pl.BlockSpec((chunk, qw), lambda b, c: (b * nc + c, 0)),
                  pl.BlockSpec((chunk, qw), lambda b, c: (b * nc + c, 1)),
                  pl.BlockSpec((chunk, vw), lambda b, c: (b * nc + c, 1)),
                  pl.BlockSpec((chunk, vw), lambda b, c: (b * nc + c, 2)),
                  pl.BlockSpec((H_C, chunk, chunk), whole3),
                  pl.BlockSpec((H_C, chunk, 1), whole3),
                  pl.BlockSpec((H_C, chunk, 1), whole3),
                  pl.BlockSpec((H_C, 1, 1), whole3)],
        out_specs=(pl.BlockSpec((chunk, vw), lambda b, c: (b * nc + c, 0)),
                   pl.BlockSpec((None, H_C, DK_C, DV_C), lambda b, c: (b, 0, 0, 0))),
        scratch_shapes=[pltpu.VMEM((H_C, DK_C, DV_C), F32)],
        compiler_params=_cparams(("parallel", "arbitrary")),
        name="ret_prompt",
    )(qkvg, qkvg, qkvg, qkvg, di, qd, kd, cd)


def _ret_sample_kernel(q_ref, k_ref, v_ref, g_ref, s_ref, di_ref, qd_ref, kd_ref, cd_ref, o_ref, so_ref):
    for h in range(H_C):
        o, s_new = _ret_step(q_ref[:, h * DK_C:(h + 1) * DK_C], k_ref[:, h * DK_C:(h + 1) * DK_C],
                             v_ref[:, h * DV_C:(h + 1) * DV_C], s_ref[h],
                             di_ref[h], qd_ref[h], kd_ref[h], cd_ref[h])
        so_ref[h] = s_new
        o_ref[:, h * DV_C:(h + 1) * DV_C] = _head_norm_gate(o, g_ref[:, h * DV_C:(h + 1) * DV_C])


def _ret_sample(qkvg, state, ic, row0, dec_b, dec_seq):
    di, qd, kd, cd = _ret_tables(dec_seq)
    blk0 = row0 // dec_seq
    qw = H_C * DK_C
    vw = H_C * DV_C
    whole3 = lambda s: (0, 0, 0)
    return pl.pallas_call(
        _ret_sample_kernel,
        out_shape=(jax.ShapeDtypeStruct((dec_b * dec_seq, vw), F32),
                   jax.ShapeDtypeStruct((dec_b, H_C, DK_C, DV_C), F32)),
        grid=(dec_b,),
        in_specs=[pl.BlockSpec((dec_seq, qw), lambda s: (blk0 + s, 0)),
                  pl.BlockSpec((dec_seq, qw), lambda s: (blk0 + s, 1)),
                  pl.BlockSpec((dec_seq, vw), lambda s: (blk0 + s, 1)),
                  pl.BlockSpec((dec_seq, vw), lambda s: (blk0 + s, 2)),
                  pl.BlockSpec((None, None, H_C, DK_C, DV_C), lambda s: (ic, s, 0, 0, 0)),
                  pl.BlockSpec((H_C, dec_seq, dec_seq), whole3),
                  pl.BlockSpec((H_C, dec_seq, 1), whole3),
                  pl.BlockSpec((H_C, dec_seq, 1), whole3),
                  pl.BlockSpec((H_C, 1, 1), whole3)],
        out_specs=(pl.BlockSpec((dec_seq, vw), lambda s: (s, 0)),
                   pl.BlockSpec((None, H_C, DK_C, DV_C), lambda s: (s, 0, 0, 0))),
        compiler_params=_cparams(("arbitrary",)),
        name="ret_sample",
    )(qkvg, qkvg, qkvg, qkvg, state, di, qd, kd, cd)


def _route(logits):
    rows = [logits[e:e + 1, :] for e in range(N_EXPERTS)]
    mx = functools.reduce(jnp.maximum, rows)
    ex = [jnp.exp(r - mx) for r in rows]
    den = functools.reduce(lambda a, b: a + b, ex)
    probs = [e / den for e in ex]
    return probs


def _pick_top2(probs, bias_ref):
    sel = [probs[e] + bias_ref[e:e + 1, :] for e in range(N_EXPERTS)]
    epg = EXPERTS_PER_GROUP
    neg = jnp.full_like(sel[0], -jnp.inf)
    gscore = []
    for g in range(N_GROUPS):
        grp = sel[g * epg:(g + 1) * epg]
        pairs = [grp[a] + grp[b] for a in range(epg) for b in range(a + 1, epg)]
        gscore.append(functools.reduce(jnp.maximum, pairs))
    best = functools.reduce(jnp.maximum, gscore)
    g_idx = jnp.full(best.shape, N_GROUPS - 1, jnp.int32)
    for g in range(N_GROUPS - 2, -1, -1):
        g_idx = jnp.where(gscore[g] == best, g, g_idx)
    ing = []
    inp = []
    for j in range(epg):
        sv = sel[j]
        pv = probs[j]
        for g in range(1, N_GROUPS):
            sv = jnp.where(g_idx == g, sel[g * epg + j], sv)
            pv = jnp.where(g_idx == g, probs[g * epg + j], pv)
        ing.append(sv)
        inp.append(pv)
    top1 = functools.reduce(jnp.maximum, ing)
    l1 = jnp.full(best.shape, epg - 1, jnp.int32)
    for j in range(epg - 2, -1, -1):
        l1 = jnp.where(ing[j] == top1, j, l1)
    rest = [jnp.where(l1 == j, neg, ing[j]) for j in range(epg)]
    top2 = functools.reduce(jnp.maximum, rest)
    l2 = jnp.full(best.shape, epg - 1, jnp.int32)
    for j in range(epg - 2, -1, -1):
        l2 = jnp.where(jnp.logical_and(rest[j] == top2, l1 != j), j, l2)
    p1 = functools.reduce(lambda a, b: a + b, [jnp.where(l1 == j, inp[j], 0.0) for j in range(epg)])
    p2 = functools.reduce(lambda a, b: a + b, [jnp.where(l2 == j, inp[j], 0.0) for j in range(epg)])
    tot = p1 + p2
    return g_idx * epg + l1, g_idx * epg + l2, p1 / tot, p2 / tot


def _outproj_kernel(*refs, n_in, n_prompt_tiles):
    p_refs = refs[:n_in]
    s_refs = refs[n_in:2 * n_in]
    x_ref, w_ref, lg_ref, lb_ref, wr_ref, br_ref, o_ref, eidx_ref, ew_ref, a_scr = refs[2 * n_in:]
    i = pl.program_id(0)

    def stage(part_refs):
        col = 0
        for r in part_refs:
            a_scr[:, col:col + r.shape[1]] = r[...].astype(BF16)
            col += r.shape[1]

    @pl.when(i < n_prompt_tiles)
    def _():
        stage(p_refs)

    @pl.when(i >= n_prompt_tiles)
    def _():
        stage(s_refs)

    mix = jnp.dot(a_scr[...], w_ref[...], preferred_element_type=F32)
    x1 = _layer_norm_rows(DN_ALPHA * x_ref[...] + mix, lg_ref[...], lb_ref[...])
    o_ref[...] = x1
    logits = lax.dot_general(wr_ref[...], x1, (((1,), (1,)), ((), ())),
                             precision=lax.Precision.HIGHEST, preferred_element_type=F32)
    probs = _route(logits)
    e1, e2, w1, w2 = _pick_top2(probs, br_ref)
    eidx_ref[0:1, :] = e1
    eidx_ref[1:2, :] = e2
    ew_ref[0:1, :] = w1
    ew_ref[1:2, :] = w2


def _outproj(parts_p, parts_s, x, w, li, lg, lb, w_router_t, b_router, tm, name):
    t_all = x.shape[0]
    n_pt = parts_p[0].shape[0] // tm
    kern = functools.partial(_outproj_kernel, n_in=len(parts_p), n_prompt_tiles=n_pt)
    vec = lambda i: (0, 0)
    return pl.pallas_call(
        kern,
        out_shape=(jax.ShapeDtypeStruct((t_all, D_MODEL), F32),
                   jax.ShapeDtypeStruct((2, t_all), jnp.int32),
                   jax.ShapeDtypeStruct((2, t_all), F32)),
        grid=(t_all // tm,),
        in_specs=[pl.BlockSpec((tm, p.shape[1]), lambda i: (jnp.minimum(i, n_pt - 1), 0)) for p in parts_p]
                 + [pl.BlockSpec((tm, p.shape[1]), lambda i: (jnp.maximum(i - n_pt, 0), 0)) for p in parts_s]
                 + [pl.BlockSpec((tm, D_MODEL), lambda i: (i, 0)),
                    pl.BlockSpec((None,) + w.shape[1:], lambda i: (li, 0, 0)),
                    pl.BlockSpec((1, D_MODEL), vec),
                    pl.BlockSpec((1, D_MODEL), vec),
                    pl.BlockSpec((N_EXPERTS, D_MODEL), vec),
                    pl.BlockSpec((N_EXPERTS, 1), vec)],
        out_specs=(pl.BlockSpec((tm, D_MODEL), lambda i: (i, 0)),
                   pl.BlockSpec((2, tm), lambda i: (0, i)),
                   pl.BlockSpec((2, tm), lambda i: (0, i))),
        scratch_shapes=[pltpu.VMEM((tm, w.shape[1]), BF16)],
        compiler_params=_cparams(("parallel",)),
        name=name,
    )(*parts_p, *parts_s, x, w, lg.reshape(1, -1), lb.reshape(1, -1), w_router_t, b_router.reshape(-1, 1))


def _moe_plan(e_idx, tile):
    n_assign = e_idx.size
    e = e_idx.reshape(n_assign)
    onehot = (e[:, None] == jnp.arange(N_EXPERTS, dtype=jnp.int32)[None, :]).astype(jnp.int32)
    csum = jnp.cumsum(onehot, axis=0)
    counts = csum[-1]
    ends = jnp.cumsum(counts)
    starts = ends - counts
    pos = jnp.sum(onehot * (starts[None, :] + csum - 1), axis=1)
    n_tiles = n_assign // tile
    n_steps = n_tiles + N_EXPERTS - 1
    first_tile = starts // tile
    tiles_e = jnp.where(counts > 0, (ends + tile - 1) // tile - first_tile, 0)
    step_end = jnp.cumsum(tiles_e)
    step_start = step_end - tiles_e
    total = step_end[-1]
    sidx = jnp.arange(n_steps, dtype=jnp.int32)
    valid = sidx < total
    s_eff = jnp.minimum(sidx, total - 1)
    exp_s = jnp.sum((step_end[None, :] <= s_eff[:, None]).astype(jnp.int32), axis=1)
    exp_s = jnp.minimum(exp_s, N_EXPERTS - 1)
    tile_s = first_tile[exp_s] + (s_eff - step_start[exp_s])
    lo = jnp.maximum(starts[exp_s], tile_s * tile) - tile_s * tile
    hi = jnp.minimum(ends[exp_s], (tile_s + 1) * tile) - tile_s * tile
    prev_tile = jnp.concatenate([jnp.full((1,), -1, jnp.int32), tile_s[:-1]])
    first = jnp.logical_and(valid, tile_s != prev_tile)
    i32 = lambda v: v.astype(jnp.int32)
    return i32(pos), (i32(tile_s), exp_s, i32(lo), i32(hi), i32(first), i32(valid))


def _rows_to_tiles(tile_ref, rows):
    n = rows.shape[0]
    for j in range(rows.shape[1] // LANES):
        tile_ref[pl.ds(j, n, stride=SUBLANES), :] = rows[:, j * LANES:(j + 1) * LANES]


def _tiles_to_rows(tile_ref, start, n):
    return jnp.concatenate([tile_ref[pl.ds(start * SUBLANES + j, n, stride=SUBLANES), :]
                            for j in range(D_MODEL // LANES)], axis=1)


def _token_tile(ref, t):
    return ref.at[pl.ds(pl.multiple_of(t * SUBLANES, SUBLANES), SUBLANES), :]


def _dispatch_kernel(pos_ref, x_ref, xs_hbm, xt_scr, sem, *, tc):
    _rows_to_tiles(xt_scr, x_ref[...])

    def row_copy(k, r):
        return pltpu.make_async_copy(_token_tile(xt_scr, r), _token_tile(xs_hbm, pos_ref[k, r]), sem)

    for k in range(2):
        lax.fori_loop(0, tc, lambda r, c: (row_copy(k, r).start(), c)[1], 0, unroll=8)
    for k in range(2):
        lax.fori_loop(0, tc, lambda r, c: (row_copy(k, r).wait(), c)[1], 0, unroll=8)


def _dispatch(x, pos3, tc):
    n_tok, d = x.shape
    kern = functools.partial(_dispatch_kernel, tc=tc)
    return pl.pallas_call(
        kern,
        out_shape=jax.ShapeDtypeStruct((2 * n_tok * SUBLANES, LANES), x.dtype),
        grid=(n_tok // tc,),
        in_specs=[pl.BlockSpec((None, 2, tc), lambda i: (i, 0, 0), memory_space=pltpu.SMEM),
                  pl.BlockSpec((tc, d), lambda i: (i, 0))],
        out_specs=pl.BlockSpec(memory_space=pl.ANY),
        scratch_shapes=[pltpu.VMEM((tc * SUBLANES, LANES), x.dtype), pltpu.SemaphoreType.DMA],
        compiler_params=_cparams(("arbitrary",)),
        name="moe_dispatch",
    )(pos3, x)


def _gmm_kernel(tile_ref, exp_ref, lo_ref, hi_ref, first_ref, valid_ref, xs_ref, wg_ref, wu_ref, wd_ref,
                o_ref, wg_b, wu_b, wd_b, xb_scr, *, tile):
    s = pl.program_id(0)
    prev = jnp.maximum(s - 1, 0)
    new_expert = jnp.logical_or(s == 0, exp_ref[s] != exp_ref[prev])

    @pl.when(new_expert)
    def _():
        wg_b[...] = wg_ref[...].astype(BF16)
        wu_b[...] = wu_ref[...].astype(BF16)
        wd_b[...] = wd_ref[...].astype(BF16)

    @pl.when(valid_ref[s] == 1)
    def _():
        xb_scr[...] = _tiles_to_rows(xs_ref, 0, tile).astype(BF16)
        xb = xb_scr[...]
        hid = _silu(jnp.dot(xb, wg_b[...], preferred_element_type=F32)) * \
            jnp.dot(xb, wu_b[...], preferred_element_type=F32)
        y = jnp.dot(hid.astype(BF16), wd_b[...], preferred_element_type=F32)
        rows = lax.broadcasted_iota(jnp.int32, (tile, 1), 0)
        mine = jnp.logical_and(rows >= lo_ref[s], rows < hi_ref[s])

        @pl.when(first_ref[s] == 1)
        def _():
            _rows_to_tiles(o_ref, jnp.where(mine, y, 0.0))

        @pl.when(first_ref[s] == 0)
        def _():
            _rows_to_tiles(o_ref, jnp.where(mine, y, _tiles_to_rows(o_ref, 0, tile)))


def _gmm(xs, plan, wg, wu, wd, layer, tile):
    d = D_MODEL
    n_steps = plan[0].shape[0]
    kern = functools.partial(_gmm_kernel, tile=tile)
    grid_spec = pltpu.PrefetchScalarGridSpec(
        num_scalar_prefetch=len(plan),
        grid=(n_steps,),
        in_specs=[pl.BlockSpec((tile * SUBLANES, LANES), lambda s, t, e, *_: (t[s], 0)),
                  pl.BlockSpec((None, None, d, D_FF), lambda s, t, e, *_: (layer, e[s], 0, 0)),
                  pl.BlockSpec((None, None, d, D_FF), lambda s, t, e, *_: (layer, e[s], 0, 0)),
                  pl.BlockSpec((None, None, D_FF, d), lambda s, t, e, *_: (layer, e[s], 0, 0))],
        out_specs=pl.BlockSpec((tile * SUBLANES, LANES), lambda s, t, e, *_: (t[s], 0)),
        scratch_shapes=[pltpu.VMEM((d, D_FF), BF16), pltpu.VMEM((d, D_FF), BF16), pltpu.VMEM((D_FF, d), BF16),
                        pltpu.VMEM((tile, d), BF16)],
    )
    return pl.pallas_call(
        kern,
        out_shape=jax.ShapeDtypeStruct(xs.shape, F32),
        grid_spec=grid_spec,
        compiler_params=_cparams(("arbitrary",)),
        name="moe_gmm",
    )(*plan, xs, wg, wu, wd)


def _combine_kernel(pos_ref, posn_ref, w_ref, x_ref, lg_ref, lb_ref, ys_hbm, o_ref, buf, sem, *, tc):
    i = pl.program_id(0)
    n = pl.num_programs(0)
    slot = i % 2

    def row_copy(p_ref, k, r, sl):
        return pltpu.make_async_copy(_token_tile(ys_hbm, p_ref[k, r]), _token_tile(buf.at[sl, k], r), sem.at[sl])

    def start_tile(p_ref, sl):
        for k in range(2):
            lax.fori_loop(0, tc, lambda r, c: (row_copy(p_ref, k, r, sl).start(), c)[1], 0, unroll=8)

    @pl.when(i == 0)
    def _():
        start_tile(pos_ref, 0)

    @pl.when(i + 1 < n)
    def _():
        start_tile(posn_ref, 1 - slot)

    for k in range(2):
        lax.fori_loop(0, tc, lambda r, c: (row_copy(pos_ref, k, r, slot).wait(), c)[1], 0, unroll=8)

    w = w_ref[...]
    sub = lax.broadcasted_iota(jnp.int32, (LANES, LANES), 0)
    for c0 in range(0, tc, LANES):
        wsq = jnp.where(sub == 0, w[0:1, c0:c0 + LANES], jnp.where(sub == 1, w[1:2, c0:c0 + LANES], 0.0))
        wt = wsq.T
        y = wt[:, 0:1] * _tiles_to_rows(buf.at[slot, 0], c0, LANES) + \
            wt[:, 1:2] * _tiles_to_rows(buf.at[slot, 1], c0, LANES)
        o_ref[c0:c0 + LANES, :] = _layer_norm_rows(DN_ALPHA * x_ref[c0:c0 + LANES, :] + y,
                                                   lg_ref[...], lb_ref[...])


def _combine(ys, pos3, ew, x, lg, lb, tc):
    n_tok, d = x.shape
    n_t = n_tok // tc
    kern = functools.partial(_combine_kernel, tc=tc)
    vec = lambda i: (0, 0)
    return pl.pallas_call(
        kern,
        out_shape=jax.ShapeDtypeStruct((n_tok, d), F32),
        grid=(n_t,),
        in_specs=[pl.BlockSpec((None, 2, tc), lambda i: (i, 0, 0), memory_space=pltpu.SMEM),
                  pl.BlockSpec((None, 2, tc), lambda i: (jnp.minimum(i + 1, n_t - 1), 0, 0),
                               memory_space=pltpu.SMEM),
                  pl.BlockSpec((2, tc), lambda i: (0, i)),
                  pl.BlockSpec((tc, d), lambda i: (i, 0)),
                  pl.BlockSpec((1, d), vec),
                  pl.BlockSpec((1, d), vec),
                  pl.BlockSpec(memory_space=pl.ANY)],
        out_specs=pl.BlockSpec((tc, d), lambda i: (i, 0)),
        scratch_shapes=[pltpu.VMEM((2, 2, tc * SUBLANES, LANES), F32), pltpu.SemaphoreType.DMA((2,))],
        compiler_params=_cparams(("arbitrary",)),
        name="moe_combine",
    )(pos3, pos3, ew, x, lg.reshape(1, -1), lb.reshape(1, -1), ys)


def _moe(x, e_idx, ew, wg, wu, wd, layer, lg, lb):
    pos, plan = _moe_plan(e_idx, MOE_TILE)
    n_t = x.shape[0] // MOE_TILE
    pos3 = pos.reshape(2, n_t, MOE_TILE).transpose(1, 0, 2)
    xs = _dispatch(x, pos3, MOE_TILE)
    ys = _gmm(xs, plan, wg, wu, wd, layer, MOE_TILE)
    return _combine(ys, pos3, ew, x, lg, lb, MOE_TILE)


def kernel(x_prompt, x_sample, cache_k, cache_v, page_table, state_conv, state_ret, w_in_a, lambda_a, subln_a,
           conv_w, conv_b, conv_ln_g, conv_ln_b, w_out_a, w_in_c, w_out_c, ln_g, ln_b, w_router, b_router,
           w_e_gate, w_e_up, w_e_down):
    bsz, seq, _ = x_prompt.shape
    dec_b, dec_seq, _ = x_sample.shape
    n_pages = page_table.shape[1]
    past_len = n_pages * PAGE_SIZE
    tp = bsz * seq
    ts_ = dec_b * dec_seq
    tm = min(512, seq)
    tm_in = min(1024, seq)
    assert seq % tm_in == 0 and ts_ % tm_in == 0 and dec_seq == SUBLANES
    seq_tiles = seq // tm_in
    n_prompt_tiles = tp // tm_in

    pos_p = jnp.arange(seq, dtype=jnp.int32)
    pos_s = past_len + (jnp.arange(tm_in, dtype=jnp.int32) % dec_seq)
    pos_tab = jnp.concatenate([pos_p, pos_s])
    tabs_a = _rot_tables_a(pos_tab)
    tabs_c = _rot_tables_c(pos_tab)

    n_phys = cache_k.shape[1]
    ck = jnp.transpose(cache_k, (0, 1, 3, 4, 2))
    cv = cache_v.reshape(cache_v.shape[0], n_phys, PAGE_SIZE * H_A, D_VA)
    w_router_t = w_router.T
    w_in_a, w_out_a, w_in_c, w_out_c = (w.astype(BF16) for w in (w_in_a, w_out_a, w_in_c, w_out_c))

    x = jnp.concatenate([x_prompt.reshape(tp, D_MODEL), x_sample.reshape(ts_, D_MODEL)], axis=0)
    k_p, v_p, conv_p, ret_p, k_s, v_s, conv_s, ret_s = [], [], [], [], [], [], [], []
    for l in range(DEPTH):
        if l % 2 == 0:
            ia = l // 2
            lam_init = 0.8 - 0.6 * math.exp(-0.3 * l)
            qkvag = _inproj(x, w_in_a, ia, tabs_a, _inproj_a_kernel, tm_in, seq_tiles, n_prompt_tiles, "inproj_a")
            att_p = _attn_prompt(qkvag, lambda_a[ia], subln_a[ia], bsz, seq, lam_init, min(256, seq))
            att_s = _attn_sample(qkvag, ck, cv, ia, page_table, lambda_a[ia], subln_a[ia], tp, dec_b,
                                 dec_seq, lam_init)
            c_p, st_p = _conv_prompt(qkvag, conv_w[ia], conv_b[ia], conv_ln_g[ia], conv_ln_b[ia], bsz, seq,
                                     min(256, seq))
            c_s, st_s = _conv_sample(qkvag, state_conv, ia, conv_w[ia], conv_b[ia], conv_ln_g[ia],
                                     conv_ln_b[ia], tp, dec_b, dec_seq, 8)
            x, e_idx, ew = _outproj([att_p, c_p], [att_s, c_s], x, w_out_a, ia, ln_g[l, 0], ln_b[l, 0], w_router_t,
                                    b_router, tm, "outproj_a")
            k_all = qkvag[:, QK_W:2 * QK_W]
            v_all = qkvag[:, 2 * QK_W:2 * QK_W + ATT_W]
            k_p.append(k_all[:tp].reshape(bsz, seq, 2 * H_A, D_QK))
            v_p.append(v_all[:tp].reshape(bsz, seq, H_A, D_VA))
            k_s.append(k_all[tp:].reshape(dec_b, dec_seq, 2 * H_A, D_QK))
            v_s.append(v_all[tp:].reshape(dec_b, dec_seq, H_A, D_VA))
            conv_p.append(st_p)
            conv_s.append(st_s)
        else:
            ic = l // 2
            qkvg = _inproj(x, w_in_c, ic, tabs_c, _inproj_c_kernel, tm_in, seq_tiles, n_prompt_tiles, "inproj_c")
            og_p, s_p = _ret_prompt(qkvg, bsz, seq, min(256, seq))
            og_s, s_s = _ret_sample(qkvg, state_ret, ic, tp, dec_b, dec_seq)
            x, e_idx, ew = _outproj([og_p], [og_s], x, w_out_c, ic, ln_g[l, 0], ln_b[l, 0], w_router_t, b_router,
                                    tm, "outproj_c")
            ret_p.append(s_p)
            ret_s.append(s_s)
        x = _moe(x, e_idx, ew, w_e_gate, w_e_up, w_e_down, l, ln_g[l, 1], ln_b[l, 1])

    y_prompt = x[:tp].reshape(bsz, seq, D_MODEL)
    y_sample = x[tp:].reshape(dec_b, dec_seq, D_MODEL)
    return (y_prompt, y_sample, jnp.stack(k_p), jnp.stack(v_p), jnp.stack(conv_p), jnp.stack(ret_p),
            jnp.stack(k_s), jnp.stack(v_s), jnp.stack(conv_s), jnp.stack(ret_s))
```

```python
import functools
import math

import jax
import jax.numpy as jnp
from jax import lax
from jax.experimental import pallas as pl
from jax.experimental.pallas import tpu as pltpu

F32 = jnp.float32
BF16 = jnp.bfloat16

D_MODEL = 1024
DEPTH = 4
PAGE_SIZE = 128
H_A = 4
D_QK = 64
D_VA = 2 * D_QK
QK_W = H_A * 2 * D_QK
ATT_W = H_A * D_VA
ROT_DIM = D_QK // 4
ROPE_THETA = 500000.0
CONV_CH = D_MODEL // 2
CONV_W = 31
H_C = 4
DK_C = D_MODEL // H_C
DV_C = 2 * DK_C
RET_THETA = 10000.0
N_EXPERTS = 16
N_GROUPS = 4
EXPERTS_PER_GROUP = N_EXPERTS // N_GROUPS
D_FF = D_MODEL // 2
DN_ALPHA = (2 * DEPTH) ** 0.25
LN_EPS = 1e-5
IN_A = 2 * QK_W + ATT_W + 2 * CONV_CH
IN_C = 2 * H_C * DK_C + 2 * H_C * DV_C

LANES = 128
SUBLANES = 8
CONV_HALO = 32
VMEM_LIMIT = 56 * 1024 * 1024
MOE_TILE = 256


def _cparams(sem):
    return pltpu.CompilerParams(dimension_semantics=sem, vmem_limit_bytes=VMEM_LIMIT)


def _bdot(a, b):
    return jnp.dot(a.astype(BF16), b.astype(BF16), preferred_element_type=F32)


def _bdot_nt(a, b):
    return lax.dot_general(a.astype(BF16), b.astype(BF16), (((1,), (1,)), ((), ())),
                           preferred_element_type=F32)


def _layer_norm_rows(v, g, b):
    mu = jnp.mean(v, -1, keepdims=True)
    d = v - mu
    var = jnp.mean(d * d, -1, keepdims=True)
    return d * lax.rsqrt(var + LN_EPS) * g + b


def _silu(v):
    return v * (1.0 / (1.0 + jnp.exp(-v)))


def _sigmoid(v):
    return 1.0 / (1.0 + jnp.exp(-v))


def _cast_rows_once(x_ref, xb_ref):
    @pl.when(pl.program_id(1) == 0)
    def _():
        xb_ref[...] = x_ref[...].astype(BF16)


def _inproj_a_kernel(x_ref, w_ref, cos_ref, sa_ref, sb_ref, o_ref, xb_ref):
    j = pl.program_id(1)
    _cast_rows_once(x_ref, xb_ref)
    y = jnp.dot(xb_ref[...], w_ref[...], preferred_element_type=F32)
    tn = y.shape[1]
    rot_w = 2 * QK_W
    rot_tiles = max(rot_w // tn, 1)
    rot_cols = min(rot_w, tn)
    assert rot_tiles * rot_cols == rot_w

    @pl.when(j < rot_tiles)
    def _():
        cos = cos_ref[...]
        sa = sa_ref[...]
        sb = sb_ref[...]
        for blk in range(rot_cols // LANES):
            t = y[:, blk * LANES:(blk + 1) * LANES]
            r = (t * cos + pltpu.roll(t, ROT_DIM // 2, 1) * sa
                 + pltpu.roll(t, LANES - ROT_DIM // 2, 1) * sb)
            o_ref[:, blk * LANES:(blk + 1) * LANES] = r
        if rot_cols < tn:
            o_ref[:, rot_cols:] = y[:, rot_cols:]

    @pl.when(j >= rot_tiles)
    def _():
        o_ref[...] = y


def _rot_tables_a(pos):
    half = ROT_DIM // 2
    inv = ROPE_THETA ** (-jnp.arange(0, ROT_DIM, 2, dtype=F32) / ROT_DIM)
    ang = pos.astype(F32)[:, None] * inv[None, :]
    c, s = jnp.cos(ang), jnp.sin(ang)
    n = pos.shape[0]
    one = jnp.ones((n, D_QK - ROT_DIM), F32)
    zero = jnp.zeros((n, D_QK - ROT_DIM), F32)
    zh = jnp.zeros((n, half), F32)
    cos64 = jnp.concatenate([c, c, one], 1)
    sa64 = jnp.concatenate([zh, s, zero], 1)
    sb64 = jnp.concatenate([-s, zh, zero], 1)
    rep = LANES // D_QK
    return jnp.tile(cos64, (1, rep)), jnp.tile(sa64, (1, rep)), jnp.tile(sb64, (1, rep))


def _inproj_c_kernel(x_ref, w_ref, cos_ref, sin_ref, o_ref, xb_ref):
    j = pl.program_id(1)
    _cast_rows_once(x_ref, xb_ref)
    y = jnp.dot(xb_ref[...], w_ref[...], preferred_element_type=F32)
    assert y.shape[1] == H_C * DK_C

    @pl.when(j < 2)
    def _():
        cos = cos_ref[...]
        sin = sin_ref[...]
        scale = jnp.where(j == 1, DK_C ** -0.5, 1.0).astype(F32)
        half = DK_C // 2
        for hd in range(H_C):
            t1 = y[:, hd * DK_C:hd * DK_C + half]
            t2 = y[:, hd * DK_C + half:(hd + 1) * DK_C]
            o_ref[:, hd * DK_C:hd * DK_C + half] = (t1 * cos - t2 * sin) * scale
            o_ref[:, hd * DK_C + half:(hd + 1) * DK_C] = (t2 * cos + t1 * sin) * scale

    @pl.when(j >= 2)
    def _():
        o_ref[...] = y


def _rot_tables_c(pos):
    inv = RET_THETA ** (-jnp.linspace(0.0, 1.0, DK_C // 2, dtype=F32))
    ang = pos.astype(F32)[:, None] * inv[None, :]
    return jnp.cos(ang), jnp.sin(ang)


def _inproj(x, w, li, tables, kern, tm, tn, seq_tiles, n_prompt_tiles, name):
    t_all, d = x.shape
    n = w.shape[2]
    assert n % tn == 0
    tw = tables[0].shape[1]

    def tab_map(i, j):
        return (jnp.where(i < n_prompt_tiles, i % seq_tiles, seq_tiles), 0)

    return pl.pallas_call(
        kern,
        out_shape=jax.ShapeDtypeStruct((t_all, n), F32),
        grid=(t_all // tm, n // tn),
        in_specs=[pl.BlockSpec((tm, d), lambda i, j: (i, 0)),
                  pl.BlockSpec((None, d, tn), lambda i, j: (li, 0, j))]
                 + [pl.BlockSpec((tm, tw), tab_map) for _ in tables],
        out_specs=pl.BlockSpec((tm, tn), lambda i, j: (i, j)),
        scratch_shapes=[pltpu.VMEM((tm, d), BF16)],
        compiler_params=_cparams(("parallel", "arbitrary")),
        name=name,
    )(x, w, *tables)


def _lambda_value(lam_ref, lam_init):
    lf = lam_ref[...]
    a = jnp.sum(lf[0:1, :] * lf[1:2, :], axis=-1, keepdims=True)
    b = jnp.sum(lf[2:3, :] * lf[3:4, :], axis=-1, keepdims=True)
    return jnp.exp(a) - jnp.exp(b) + lam_init


def _sub_norm(o, g, lam_init):
    ms = jnp.mean(o * o, -1, keepdims=True)
    return o * lax.rsqrt(ms + LN_EPS) * g * (1.0 - lam_init)


def _attn_prompt_kernel(q_ref, k_ref, v_ref, lam_ref, g_ref, o_ref, qt_scr, kb_scr, vt_scr, m_scr, acc_scr,
                        *, lam_init, tq):
    qi = pl.program_id(1)
    n_chain = 2 * H_A
    n_kt = k_ref.shape[0] // tq

    @pl.when(qi == 0)
    def _():
        kb_scr[...] = k_ref[...].astype(BF16)
        ones = jnp.ones((D_VA, tq), BF16)
        for j in range(n_kt):
            for h in range(H_A):
                vt_scr[j, h, 0:D_VA, :] = v_ref[j * tq:(j + 1) * tq, h * D_VA:(h + 1) * D_VA].T.astype(BF16)
                vt_scr[j, h, D_VA:2 * D_VA, :] = ones

    sub = lax.broadcasted_iota(jnp.int32, (D_VA, tq), 0)
    for h in range(H_A):
        qht = (q_ref[:, h * D_VA:(h + 1) * D_VA] * (D_QK ** -0.5)).T
        for m in range(2):
            qt_scr[2 * h + m] = jnp.where(sub // D_QK == m, qht, 0.0).astype(BF16)
    m_scr[...] = jnp.full(m_scr.shape, -jnp.inf, F32)
    acc_scr[...] = jnp.zeros(acc_scr.shape, F32)
    key = lax.broadcasted_iota(jnp.int32, (tq, tq), 0)
    qry = lax.broadcasted_iota(jnp.int32, (tq, tq), 1)

    def block(j, diagonal):
        start = pl.multiple_of(j * tq, tq)
        for c in range(n_chain):
            h = c // 2
            st = jnp.dot(kb_scr[pl.ds(start, tq), h * D_VA:(h + 1) * D_VA], qt_scr[c],
                         preferred_element_type=F32)
            if diagonal:
                st = jnp.where(key <= qry, st, -jnp.inf)
            m_prev = m_scr[c]
            m_new = jnp.maximum(m_prev, jnp.max(st, 0, keepdims=True))
            pt = jnp.exp(st - m_new).astype(BF16)
            pv = jnp.dot(vt_scr[j, h], pt, preferred_element_type=F32)
            acc_scr[c] = jnp.exp(m_prev - m_new) * acc_scr[c] + pv
            m_scr[c] = m_new

    def trip(j, carry):
        block(j, False)
        return carry

    lax.fori_loop(0, qi, trip, 0)
    block(qi, True)
    lam = _lambda_value(lam_ref, lam_init)
    for h in range(H_A):
        a0 = acc_scr[2 * h]
        a1 = acc_scr[2 * h + 1]
        ot = a0[:D_VA] / a0[D_VA:D_VA + 1] - lam * (a1[:D_VA] / a1[D_VA:D_VA + 1])
        o_ref[:, h * D_VA:(h + 1) * D_VA] = _sub_norm(ot.T, g_ref[...], lam_init)


def _attn_prompt(qkvag, lam_p, subln_g, bsz, seq, lam_init, tq):
    nq = seq // tq
    kern = functools.partial(_attn_prompt_kernel, lam_init=lam_init, tq=tq)
    return pl.pallas_call(
        kern,
        out_shape=jax.ShapeDtypeStruct((bsz * seq, ATT_W), F32),
        grid=(bsz, nq),
        in_specs=[pl.BlockSpec((tq, QK_W), lambda b, qi: (b * nq + qi, 0)),
                  pl.BlockSpec((seq, QK_W), lambda b, qi: (b, 1)),
                  pl.BlockSpec((seq, ATT_W), lambda b, qi: (b, 2)),
                  pl.BlockSpec((4, D_QK), lambda b, qi: (0, 0)),
                  pl.BlockSpec((1, D_VA), lambda b, qi: (0, 0))],
        out_specs=pl.BlockSpec((tq, ATT_W), lambda b, qi: (b * nq + qi, 0)),
        scratch_shapes=[pltpu.VMEM((2 * H_A, D_VA, tq), BF16),
                        pltpu.VMEM((seq, QK_W), BF16),
                        pltpu.VMEM((nq, H_A, 2 * D_VA, tq), BF16),
                        pltpu.VMEM((2 * H_A, 1, tq), F32),
                        pltpu.VMEM((2 * H_A, 2 * D_VA, tq), F32)],
        compiler_params=_cparams(("parallel", "arbitrary")),
        name="attn_prompt",
    )(qkvag, qkvag, qkvag, lam_p, subln_g.reshape(1, D_VA))


def _attn_sample_kernel(pt_ref, q_ref, kn_ref, vn_ref, lam_ref, g_ref, *rest, lam_init, n_pages, dec_seq):
    k_refs = rest[:n_pages]
    v_refs = rest[n_pages:2 * n_pages]
    o_ref = rest[2 * n_pages]
    nrow = 2 * H_A * dec_seq
    q = q_ref[...] * (D_QK ** -0.5)
    qt = jnp.concatenate([q] * (2 * H_A), axis=0)
    rid = lax.broadcasted_iota(jnp.int32, (nrow, QK_W), 0)
    cid = lax.broadcasted_iota(jnp.int32, (nrow, QK_W), 1)
    qbd = jnp.where(cid // D_QK == rid // dec_seq, qt, 0.0).astype(BF16)

    s_past = [_bdot(qbd, kr[...].reshape(QK_W, PAGE_SIZE)) for kr in k_refs]
    s_new = _bdot_nt(qbd, kn_ref[...])
    qpos = lax.broadcasted_iota(jnp.int32, (nrow, dec_seq), 0) % dec_seq
    kpos = lax.broadcasted_iota(jnp.int32, (nrow, dec_seq), 1)
    s_new = jnp.where(kpos <= qpos, s_new, -jnp.inf)
    m = jnp.max(s_new, -1, keepdims=True)
    for s in s_past:
        m = jnp.maximum(m, jnp.max(s, -1, keepdims=True))
    p_new = jnp.exp(s_new - m)
    l = jnp.sum(p_new, -1, keepdims=True)
    p_past = []
    for s in s_past:
        p = jnp.exp(s - m)
        l = l + jnp.sum(p, -1, keepdims=True)
        p_past.append(p.astype(BF16))
    lam = _lambda_value(lam_ref, lam_init)
    g = g_ref[...]
    grp = 2 * dec_seq
    for h in range(H_A):
        r0 = h * grp
        acc = _bdot(p_new[r0:r0 + grp], vn_ref[:, h * D_VA:(h + 1) * D_VA])
        for p, vr in zip(p_past, v_refs):
            acc = acc + _bdot(p[r0:r0 + grp], vr[pl.ds(h, PAGE_SIZE, stride=H_A), :])
        acc = acc / l[r0:r0 + grp]
        o = acc[:dec_seq] - lam * acc[dec_seq:]
        o_ref[:, h * D_VA:(h + 1) * D_VA] = _sub_norm(o, g, lam_init)


def _attn_sample(qkvag, cache_kt, cache_vr, ia, page_table, lam_p, subln_g, row0, dec_b, dec_seq, lam_init):
    n_pages = page_table.shape[1]
    kern = functools.partial(_attn_sample_kernel, lam_init=lam_init, n_pages=n_pages, dec_seq=dec_seq)
    blk0 = row0 // dec_seq

    def k_spec(p):
        return pl.BlockSpec((None, None, 2 * H_A, D_QK, PAGE_SIZE), lambda s, pt: (ia, pt[s, p], 0, 0, 0))

    def v_spec(p):
        return pl.BlockSpec((None, None, PAGE_SIZE * H_A, D_VA), lambda s, pt: (ia, pt[s, p], 0, 0))

    grid_spec = pltpu.PrefetchScalarGridSpec(
        num_scalar_prefetch=1,
        grid=(dec_b,),
        in_specs=[pl.BlockSpec((dec_seq, QK_W), lambda s, pt: (blk0 + s, 0)),
                  pl.BlockSpec((dec_seq, QK_W), lambda s, pt: (blk0 + s, 1)),
                  pl.BlockSpec((dec_seq, ATT_W), lambda s, pt: (blk0 + s, 2)),
                  pl.BlockSpec((4, D_QK), lambda s, pt: (0, 0)),
                  pl.BlockSpec((1, D_VA), lambda s, pt: (0, 0))]
                 + [k_spec(p) for p in range(n_pages)]
                 + [v_spec(p) for p in range(n_pages)],
        out_specs=pl.BlockSpec((dec_seq, ATT_W), lambda s, pt: (s, 0)),
    )
    return pl.pallas_call(
        kern,
        out_shape=jax.ShapeDtypeStruct((dec_b * dec_seq, ATT_W), F32),
        grid_spec=grid_spec,
        compiler_params=_cparams(("arbitrary",)),
        name="attn_sample",
    )(page_table, qkvag, qkvag, qkvag, lam_p, subln_g.reshape(1, D_VA),
      *([cache_kt] * n_pages), *([cache_vr] * n_pages))


def _conv_taps(scr, cw_ref, row0, rows):
    acc = None
    for w in range(CONV_W):
        term = scr[pl.ds(row0 + w, rows), :] * cw_ref[w:w + 1, :]
        acc = term if acc is None else acc + term
    return acc


def _conv_prompt_kernel(a_ref, g_ref, ah_ref, gh_ref, cw_ref, cb_ref, lg_ref, lb_ref,
                        c_ref, st_ref, scr, *, ts, chunk):
    i = pl.program_id(1)
    n = pl.num_programs(1)
    u = a_ref[...] * _sigmoid(g_ref[...])
    uh = ah_ref[...] * _sigmoid(gh_ref[...])
    uh = jnp.where(i > 0, uh, 0.0)
    scr[0:CONV_HALO, :] = uh
    scr[CONV_HALO:CONV_HALO + ts, :] = u
    off = CONV_HALO - (CONV_W - 1)
    for c0 in range(0, ts, chunk):
        c = _conv_taps(scr, cw_ref, c0 + off, chunk) + cb_ref[...]
        c_ref[c0:c0 + chunk, :] = _silu(_layer_norm_rows(c, lg_ref[...], lb_ref[...]))

    @pl.when(i == n - 1)
    def _():
        st_ref[...] = scr[CONV_HALO + ts - (CONV_W - 1):CONV_HALO + ts, :]


def _conv_prompt(qkvag, cw, cb, lg, lb, bsz, seq, ts):
    ns = seq // ts
    hb = ts // CONV_HALO
    acol = (2 * QK_W + ATT_W) // CONV_CH
    kern = functools.partial(_conv_prompt_kernel, ts=ts, chunk=32)
    vec = lambda b, i: (0, 0)
    return pl.pallas_call(
        kern,
        out_shape=(jax.ShapeDtypeStruct((bsz * seq, CONV_CH), F32),
                   jax.ShapeDtypeStruct((bsz, CONV_W - 1, CONV_CH), F32)),
        grid=(bsz, ns),
        in_specs=[pl.BlockSpec((ts, CONV_CH), lambda b, i: (b * ns + i, acol)),
                  pl.BlockSpec((ts, CONV_CH), lambda b, i: (b * ns + i, acol + 1)),
                  pl.BlockSpec((CONV_HALO, CONV_CH), lambda b, i: (jnp.maximum((b * ns + i) * hb - 1, 0), acol)),
                  pl.BlockSpec((CONV_HALO, CONV_CH), lambda b, i: (jnp.maximum((b * ns + i) * hb - 1, 0), acol + 1)),
                  pl.BlockSpec((CONV_W, CONV_CH), vec),
                  pl.BlockSpec((1, CONV_CH), vec),
                  pl.BlockSpec((1, CONV_CH), vec),
                  pl.BlockSpec((1, CONV_CH), vec)],
        out_specs=(pl.BlockSpec((ts, CONV_CH), lambda b, i: (b * ns + i, 0)),
                   pl.BlockSpec((None, CONV_W - 1, CONV_CH), lambda b, i: (b, 0, 0))),
        scratch_shapes=[pltpu.VMEM((CONV_HALO + ts, CONV_CH), F32)],
        compiler_params=_cparams(("parallel", "arbitrary")),
        name="conv_prompt",
    )(qkvag, qkvag, qkvag, qkvag, cw, cb.reshape(1, -1), lg.reshape(1, -1), lb.reshape(1, -1))


def _conv_sample_kernel(a_ref, g_ref, st_ref, cw_ref, cb_ref, lg_ref, lb_ref, c_ref, so_ref, scr, *, nb, dec_seq):
    hist = CONV_W - 1
    u = a_ref[...] * _sigmoid(g_ref[...])
    for s in range(nb):
        scr[0:hist, :] = st_ref[s]
        scr[hist:hist + dec_seq, :] = u[s * dec_seq:(s + 1) * dec_seq, :]
        c = _conv_taps(scr, cw_ref, 0, dec_seq) + cb_ref[...]
        c_ref[s * dec_seq:(s + 1) * dec_seq, :] = _silu(_layer_norm_rows(c, lg_ref[...], lb_ref[...]))
        so_ref[s] = scr[dec_seq:dec_seq + hist, :]


def _conv_sample(qkvag, state, ia, cw, cb, lg, lb, row0, dec_b, dec_seq, nb):
    acol = (2 * QK_W + ATT_W) // CONV_CH
    blk0 = row0 // (nb * dec_seq)
    kern = functools.partial(_conv_sample_kernel, nb=nb, dec_seq=dec_seq)
    vec = lambda i: (0, 0)
    return pl.pallas_call(
        kern,
        out_shape=(jax.ShapeDtypeStruct((dec_b * dec_seq, CONV_CH), F32),
                   jax.ShapeDtypeStruct((dec_b, CONV_W - 1, CONV_CH), F32)),
        grid=(dec_b // nb,),
        in_specs=[pl.BlockSpec((nb * dec_seq, CONV_CH), lambda i: (blk0 + i, acol)),
                  pl.BlockSpec((nb * dec_seq, CONV_CH), lambda i: (blk0 + i, acol + 1)),
                  pl.BlockSpec((None, nb, CONV_W - 1, CONV_CH), lambda i: (ia, i, 0, 0)),
                  pl.BlockSpec((CONV_W, CONV_CH), vec),
                  pl.BlockSpec((1, CONV_CH), vec),
                  pl.BlockSpec((1, CONV_CH), vec),
                  pl.BlockSpec((1, CONV_CH), vec)],
        out_specs=(pl.BlockSpec((nb * dec_seq, CONV_CH), lambda i: (i, 0)),
                   pl.BlockSpec((nb, CONV_W - 1, CONV_CH), lambda i: (i, 0, 0))),
        scratch_shapes=[pltpu.VMEM((CONV_W - 1 + dec_seq + SUBLANES, CONV_CH), F32)],
        compiler_params=_cparams(("arbitrary",)),
        name="conv_sample",
    )(qkvag, qkvag, state, cw, cb.reshape(1, -1), lg.reshape(1, -1), lb.reshape(1, -1))


def _ret_tables(c):
    log_g = jnp.log1p(-jnp.exp2(-5.0 - jnp.arange(H_C, dtype=F32)))
    idx = jnp.arange(c, dtype=F32)
    diff = idx[:, None] - idx[None, :]
    causal = diff >= 0
    d_intra = jnp.where(causal[None], jnp.exp(jnp.where(causal, diff, 0.0)[None] * log_g[:, None, None]), 0.0)
    q_dec = jnp.exp((idx[None, :] + 1.0) * log_g[:, None])[..., None]
    k_dec = jnp.exp((c - 1.0 - idx[None, :]) * log_g[:, None])[..., None]
    c_dec = jnp.exp(c * log_g)[:, None, None]
    return d_intra, q_dec, k_dec, c_dec


def _head_norm_gate(o, g):
    mu = jnp.mean(o, -1, keepdims=True)
    d = o - mu
    var = jnp.mean(d * d, -1, keepdims=True)
    return _silu(g) * (d * lax.rsqrt(var + LN_EPS))


def _ret_step(q, k, v, s, d_intra, q_dec, k_dec, c_dec):
    vb = v.astype(BF16)
    sc = _bdot_nt(q, k) * d_intra
    o = jnp.dot(sc.astype(BF16), vb, preferred_element_type=F32) + _bdot(q * q_dec, s)
    s_new = s * c_dec + jnp.dot((k * k_dec).T.astype(BF16), vb, preferred_element_type=F32)
    return o, s_new


def _ret_prompt_kernel(q_ref, k_ref, v_ref, g_ref, di_ref, qd_ref, kd_ref, cd_ref, o_ref, so_ref, s_scr):
    ci = pl.program_id(1)

    @pl.when(ci == 0)
    def _():
        s_scr[...] = jnp.zeros(s_scr.shape, F32)

    for h in range(H_C):
        o, s_new = _ret_step(q_ref[:, h * DK_C:(h + 1) * DK_C], k_ref[:, h * DK_C:(h + 1) * DK_C],
                             v_ref[:, h * DV_C:(h + 1) * DV_C], s_scr[h],
                             di_ref[h], qd_ref[h], kd_ref[h], cd_ref[h])
        s_scr[h] = s_new
        o_ref[:, h * DV_C:(h + 1) * DV_C] = _head_norm_gate(o, g_ref[:, h * DV_C:(h + 1) * DV_C])

    @pl.when(ci == pl.num_programs(1) - 1)
    def _():
        so_ref[...] = s_scr[...]


def _ret_prompt(qkvg, bsz, seq, chunk):
    nc = seq // chunk
    di, qd, kd, cd = _ret_tables(chunk)
    qw = H_C * DK_C
    vw = H_C * DV_C
    whole3 = lambda b, c: (0, 0, 0)
    return pl.pallas_call(
        _ret_prompt_kernel,
        out_shape=(jax.ShapeDtypeStruct((bsz * seq, vw), F32),
                   jax.ShapeDtypeStruct((bsz, H_C, DK_C, DV_C), F32)),
        grid=(bsz, nc),
        in_specs=[pl.BlockSpec((chunk, qw), lambda b, c: (b * nc + c, 0)),
                  pl.BlockSpec((chunk, qw), lambda b, c: (b * nc + c, 1)),
                  pl.BlockSpec((chunk, vw), lambda b, c: (b * nc + c, 1)),
                  pl.BlockSpec((chunk, vw), lambda b, c: (b * nc + c, 2)),
                  pl.BlockSpec((H_C, chunk, chunk), whole3),
                  pl.BlockSpec((H_C, chunk, 1), whole3),
                  pl.BlockSpec((H_C, chunk, 1), whole3),
                  pl.BlockSpec((H_C, 1, 1), whole3)],
        out_specs=(pl.BlockSpec((chunk, vw), lambda b, c: (b * nc + c, 0)),
                   pl.BlockSpec((None, H_C, DK_C, DV_C), lambda b, c: (b, 0, 0, 0))),
        scratch_shapes=[pltpu.VMEM((H_C, DK_C, DV_C), F32)],
        compiler_params=_cparams(("parallel", "arbitrary")),
        name="ret_prompt",
    )(qkvg, qkvg, qkvg, qkvg, di, qd, kd, cd)


def _ret_sample_kernel(q_ref, k_ref, v_ref, g_ref, s_ref, di_ref, qd_ref, kd_ref, cd_ref, o_ref, so_ref):
    for h in range(H_C):
        o, s_new = _ret_step(q_ref[:, h * DK_C:(h + 1) * DK_C], k_ref[:, h * DK_C:(h + 1) * DK_C],
                             v_ref[:, h * DV_C:(h + 1) * DV_C], s_ref[h],
                             di_ref[h], qd_ref[h], kd_ref[h], cd_ref[h])
        so_ref[h] = s_new
        o_ref[:, h * DV_C:(h + 1) * DV_C] = _head_norm_gate(o, g_ref[:, h * DV_C:(h + 1) * DV_C])


def _ret_sample(qkvg, state, ic, row0, dec_b, dec_seq):
    di, qd, kd, cd = _ret_tables(dec_seq)
    blk0 = row0 // dec_seq
    qw = H_C * DK_C
    vw = H_C * DV_C
    whole3 = lambda s: (0, 0, 0)
    return pl.pallas_call(
        _ret_sample_kernel,
        out_shape=(jax.ShapeDtypeStruct((dec_b * dec_seq, vw), F32),
                   jax.ShapeDtypeStruct((dec_b, H_C, DK_C, DV_C), F32)),
        grid=(dec_b,),
        in_specs=[pl.BlockSpec((dec_seq, qw), lambda s: (blk0 + s, 0)),
                  pl.BlockSpec((dec_seq, qw), lambda s: (blk0 + s, 1)),
                  pl.BlockSpec((dec_seq, vw), lambda s: (blk0 + s, 1)),
                  pl.BlockSpec((dec_seq, vw), lambda s: (blk0 + s, 2)),
                  pl.BlockSpec((None, None, H_C, DK_C, DV_C), lambda s: (ic, s, 0, 0, 0)),
                  pl.BlockSpec((H_C, dec_seq, dec_seq), whole3),
                  pl.BlockSpec((H_C, dec_seq, 1), whole3),
                  pl.BlockSpec((H_C, dec_seq, 1), whole3),
                  pl.BlockSpec((H_C, 1, 1), whole3)],
        out_specs=(pl.BlockSpec((dec_seq, vw), lambda s: (s, 0)),
                   pl.BlockSpec((None, H_C, DK_C, DV_C), lambda s: (s, 0, 0, 0))),
        compiler_params=_cparams(("arbitrary",)),
        name="ret_sample",
    )(qkvg, qkvg, qkvg, qkvg, state, di, qd, kd, cd)


def _route(logits):
    rows = [logits[e:e + 1, :] for e in range(N_EXPERTS)]
    mx = functools.reduce(jnp.maximum, rows)
    ex = [jnp.exp(r - mx) for r in rows]
    den = functools.reduce(lambda a, b: a + b, ex)
    probs = [e / den for e in ex]
    return probs


def _pick_top2(probs, bias_ref):
    sel = [probs[e] + bias_ref[e:e + 1, :] for e in range(N_EXPERTS)]
    epg = EXPERTS_PER_GROUP
    neg = jnp.full_like(sel[0], -jnp.inf)
    gscore = []
    for g in range(N_GROUPS):
        grp = sel[g * epg:(g + 1) * epg]
        pairs = [grp[a] + grp[b] for a in range(epg) for b in range(a + 1, epg)]
        gscore.append(functools.reduce(jnp.maximum, pairs))
    best = functools.reduce(jnp.maximum, gscore)
    g_idx = jnp.full(best.shape, N_GROUPS - 1, jnp.int32)
    for g in range(N_GROUPS - 2, -1, -1):
        g_idx = jnp.where(gscore[g] == best, g, g_idx)
    ing = []
    inp = []
    for j in range(epg):
        sv = sel[j]
        pv = probs[j]
        for g in range(1, N_GROUPS):
            sv = jnp.where(g_idx == g, sel[g * epg + j], sv)
            pv = jnp.where(g_idx == g, probs[g * epg + j], pv)
        ing.append(sv)
        inp.append(pv)
    top1 = functools.reduce(jnp.maximum, ing)
    l1 = jnp.full(best.shape, epg - 1, jnp.int32)
    for j in range(epg - 2, -1, -1):
        l1 = jnp.where(ing[j] == top1, j, l1)
    rest = [jnp.where(l1 == j, neg, ing[j]) for j in range(epg)]
    top2 = functools.reduce(jnp.maximum, rest)
    l2 = jnp.full(best.shape, epg - 1, jnp.int32)
    for j in range(epg - 2, -1, -1):
        l2 = jnp.where(jnp.logical_and(rest[j] == top2, l1 != j), j, l2)
    p1 = functools.reduce(lambda a, b: a + b, [jnp.where(l1 == j, inp[j], 0.0) for j in range(epg)])
    p2 = functools.reduce(lambda a, b: a + b, [jnp.where(l2 == j, inp[j], 0.0) for j in range(epg)])
    tot = p1 + p2
    return g_idx * epg + l1, g_idx * epg + l2, p1 / tot, p2 / tot


def _outproj_kernel(*refs, n_in, n_prompt_tiles):
    p_refs = refs[:n_in]
    s_refs = refs[n_in:2 * n_in]
    x_ref, w_ref, lg_ref, lb_ref, wr_ref, br_ref, o_ref, eidx_ref, ew_ref, a_scr = refs[2 * n_in:]
    i = pl.program_id(0)

    def stage(part_refs):
        col = 0
        for r in part_refs:
            a_scr[:, col:col + r.shape[1]] = r[...].astype(BF16)
            col += r.shape[1]

    @pl.when(i < n_prompt_tiles)
    def _():
        stage(p_refs)

    @pl.when(i >= n_prompt_tiles)
    def _():
        stage(s_refs)

    mix = jnp.dot(a_scr[...], w_ref[...], preferred_element_type=F32)
    x1 = _layer_norm_rows(DN_ALPHA * x_ref[...] + mix, lg_ref[...], lb_ref[...])
    o_ref[...] = x1
    logits = lax.dot_general(wr_ref[...], x1, (((1,), (1,)), ((), ())),
                             precision=lax.Precision.HIGHEST, preferred_element_type=F32)
    probs = _route(logits)
    e1, e2, w1, w2 = _pick_top2(probs, br_ref)
    eidx_ref[0:1, :] = e1
    eidx_ref[1:2, :] = e2
    ew_ref[0:1, :] = w1
    ew_ref[1:2, :] = w2


def _outproj(parts_p, parts_s, x, w, li, lg, lb, w_router_t, b_router, tm, name):
    t_all = x.shape[0]
    n_pt = parts_p[0].shape[0] // tm
    kern = functools.partial(_outproj_kernel, n_in=len(parts_p), n_prompt_tiles=n_pt)
    vec = lambda i: (0, 0)
    return pl.pallas_call(
        kern,
        out_shape=(jax.ShapeDtypeStruct((t_all, D_MODEL), F32),
                   jax.ShapeDtypeStruct((2, t_all), jnp.int32),
                   jax.ShapeDtypeStruct((2, t_all), F32)),
        grid=(t_all // tm,),
        in_specs=[pl.BlockSpec((tm, p.shape[1]), lambda i: (jnp.minimum(i, n_pt - 1), 0)) for p in parts_p]
                 + [pl.BlockSpec((tm, p.shape[1]), lambda i: (jnp.maximum(i - n_pt, 0), 0)) for p in parts_s]
                 + [pl.BlockSpec((tm, D_MODEL), lambda i: (i, 0)),
                    pl.BlockSpec((None,) + w.shape[1:], lambda i: (li, 0, 0)),
                    pl.BlockSpec((1, D_MODEL), vec),
                    pl.BlockSpec((1, D_MODEL), vec),
                    pl.BlockSpec((N_EXPERTS, D_MODEL), vec),
                    pl.BlockSpec((N_EXPERTS, 1), vec)],
        out_specs=(pl.BlockSpec((tm, D_MODEL), lambda i: (i, 0)),
                   pl.BlockSpec((2, tm), lambda i: (0, i)),
                   pl.BlockSpec((2, tm), lambda i: (0, i))),
        scratch_shapes=[pltpu.VMEM((tm, w.shape[1]), BF16)],
        compiler_params=_cparams(("parallel",)),
        name=name,
    )(*parts_p, *parts_s, x, w, lg.reshape(1, -1), lb.reshape(1, -1), w_router_t, b_router.reshape(-1, 1))


def _moe_plan(e_idx, tile):
    n_assign = e_idx.size
    e = e_idx.reshape(n_assign)
    onehot = (e[:, None] == jnp.arange(N_EXPERTS, dtype=jnp.int32)[None, :]).astype(jnp.int32)
    csum = jnp.cumsum(onehot, axis=0)
    counts = csum[-1]
    ends = jnp.cumsum(counts)
    starts = ends - counts
    pos = jnp.sum(onehot * (starts[None, :] + csum - 1), axis=1)
    n_tiles = n_assign // tile
    n_steps = n_tiles + N_EXPERTS - 1
    first_tile = starts // tile
    tiles_e = jnp.where(counts > 0, (ends + tile - 1) // tile - first_tile, 0)
    step_end = jnp.cumsum(tiles_e)
    step_start = step_end - tiles_e
    total = step_end[-1]
    sidx = jnp.arange(n_steps, dtype=jnp.int32)
    valid = sidx < total
    s_eff = jnp.minimum(sidx, total - 1)
    exp_s = jnp.sum((step_end[None, :] <= s_eff[:, None]).astype(jnp.int32), axis=1)
    exp_s = jnp.minimum(exp_s, N_EXPERTS - 1)
    tile_s = first_tile[exp_s] + (s_eff - step_start[exp_s])
    lo = jnp.maximum(starts[exp_s], tile_s * tile) - tile_s * tile
    hi = jnp.minimum(ends[exp_s], (tile_s + 1) * tile) - tile_s * tile
    prev_tile = jnp.concatenate([jnp.full((1,), -1, jnp.int32), tile_s[:-1]])
    first = jnp.logical_and(valid, tile_s != prev_tile)
    i32 = lambda v: v.astype(jnp.int32)
    return i32(pos), (i32(tile_s), exp_s, i32(lo), i32(hi), i32(first), i32(valid))


def _rows_to_tiles(tile_ref, rows):
    n = rows.shape[0]
    for j in range(rows.shape[1] // LANES):
        tile_ref[pl.ds(j, n, stride=SUBLANES), :] = rows[:, j * LANES:(j + 1) * LANES]


def _tiles_to_rows(tile_ref, start, n):
    return jnp.concatenate([tile_ref[pl.ds(start * SUBLANES + j, n, stride=SUBLANES), :]
                            for j in range(D_MODEL // LANES)], axis=1)


def _token_tile(ref, t):
    return ref.at[pl.ds(pl.multiple_of(t * SUBLANES, SUBLANES), SUBLANES), :]


def _dispatch_kernel(pos_ref, posp_ref, x_ref, xs_hbm, xt_scr, sem, *, tc):
    i = pl.program_id(0)
    n = pl.num_programs(0)
    slot = i % 2
    _rows_to_tiles(xt_scr.at[slot], x_ref[...])

    def row_copy(p_ref, k, r, sl):
        return pltpu.make_async_copy(_token_tile(xt_scr.at[sl], r), _token_tile(xs_hbm, p_ref[k, r]), sem.at[sl])

    def wait_tile(p_ref, sl):
        for k in range(2):
            lax.fori_loop(0, tc, lambda r, c: (row_copy(p_ref, k, r, sl).wait(), c)[1], 0, unroll=8)

    for k in range(2):
        lax.fori_loop(0, tc, lambda r, c: (row_copy(pos_ref, k, r, slot).start(), c)[1], 0, unroll=8)

    @pl.when(i > 0)
    def _():
        wait_tile(posp_ref, 1 - slot)

    @pl.when(i == n - 1)
    def _():
        wait_tile(pos_ref, slot)


def _dispatch(x, pos3, tc):
    n_tok, d = x.shape
    kern = functools.partial(_dispatch_kernel, tc=tc)
    return pl.pallas_call(
        kern,
        out_shape=jax.ShapeDtypeStruct((2 * n_tok * SUBLANES, LANES), x.dtype),
        grid=(n_tok // tc,),
        in_specs=[pl.BlockSpec((None, 2, tc), lambda i: (i, 0, 0), memory_space=pltpu.SMEM),
                  pl.BlockSpec((None, 2, tc), lambda i: (jnp.maximum(i - 1, 0), 0, 0), memory_space=pltpu.SMEM),
                  pl.BlockSpec((tc, d), lambda i: (i, 0))],
        out_specs=pl.BlockSpec(memory_space=pl.ANY),
        scratch_shapes=[pltpu.VMEM((2, tc * SUBLANES, LANES), x.dtype), pltpu.SemaphoreType.DMA((2,))],
        compiler_params=_cparams(("arbitrary",)),
        name="moe_dispatch",
    )(pos3, pos3, x)


def _gmm_kernel(tile_ref, exp_ref, lo_ref, hi_ref, first_ref, valid_ref, xs_ref, wg_ref, wu_ref, wd_ref,
                o_ref, wg_b, wu_b, wd_b, xb_scr, *, tile):
    s = pl.program_id(0)
    prev = jnp.maximum(s - 1, 0)
    new_expert = jnp.logical_or(s == 0, exp_ref[s] != exp_ref[prev])

    @pl.when(new_expert)
    def _():
        wg_b[...] = wg_ref[...].astype(BF16)
        wu_b[...] = wu_ref[...].astype(BF16)
        wd_b[...] = wd_ref[...].astype(BF16)

    @pl.when(valid_ref[s] == 1)
    def _():
        xb_scr[...] = _tiles_to_rows(xs_ref, 0, tile).astype(BF16)
        xb = xb_scr[...]
        hid = _silu(jnp.dot(xb, wg_b[...], preferred_element_type=F32)) * \
            jnp.dot(xb, wu_b[...], preferred_element_type=F32)
        y = jnp.dot(hid.astype(BF16), wd_b[...], preferred_element_type=F32)
        rows = lax.broadcasted_iota(jnp.int32, (tile, 1), 0)
        mine = jnp.logical_and(rows >= lo_ref[s], rows < hi_ref[s])

        @pl.when(first_ref[s] == 1)
        def _():
            _rows_to_tiles(o_ref, jnp.where(mine, y, 0.0))

        @pl.when(first_ref[s] == 0)
        def _():
            _rows_to_tiles(o_ref, jnp.where(mine, y, _tiles_to_rows(o_ref, 0, tile)))


def _gmm(xs, plan, wg, wu, wd, layer, tile):
    d = D_MODEL
    n_steps = plan[0].shape[0]
    kern = functools.partial(_gmm_kernel, tile=tile)
    grid_spec = pltpu.PrefetchScalarGridSpec(
        num_scalar_prefetch=len(plan),
        grid=(n_steps,),
        in_specs=[pl.BlockSpec((tile * SUBLANES, LANES), lambda s, t, e, *_: (t[s], 0)),
                  pl.BlockSpec((None, None, d, D_FF), lambda s, t, e, *_: (layer, e[s], 0, 0)),
                  pl.BlockSpec((None, None, d, D_FF), lambda s, t, e, *_: (layer, e[s], 0, 0)),
                  pl.BlockSpec((None, None, D_FF, d), lambda s, t, e, *_: (layer, e[s], 0, 0))],
        out_specs=pl.BlockSpec((tile * SUBLANES, LANES), lambda s, t, e, *_: (t[s], 0)),
        scratch_shapes=[pltpu.VMEM((d, D_FF), BF16), pltpu.VMEM((d, D_FF), BF16), pltpu.VMEM((D_FF, d), BF16),
                        pltpu.VMEM((tile, d), BF16)],
    )
    return pl.pallas_call(
        kern,
        out_shape=jax.ShapeDtypeStruct(xs.shape, F32),
        grid_spec=grid_spec,
        compiler_params=_cparams(("arbitrary",)),
        name="moe_gmm",
    )(*plan, xs, wg, wu, wd)


def _combine_kernel(pos_ref, posn_ref, w_ref, x_ref, lg_ref, lb_ref, ys_hbm, o_ref, buf, sem, *, tc):
    i = pl.program_id(0)
    n = pl.num_programs(0)
    slot = i % 2

    def row_copy(p_ref, k, r, sl):
        return pltpu.make_async_copy(_token_tile(ys_hbm, p_ref[k, r]), _token_tile(buf.at[sl, k], r), sem.at[sl])

    def start_tile(p_ref, sl):
        for k in range(2):
            lax.fori_loop(0, tc, lambda r, c: (row_copy(p_ref, k, r, sl).start(), c)[1], 0, unroll=8)

    @pl.when(i == 0)
    def _():
        start_tile(pos_ref, 0)

    @pl.when(i + 1 < n)
    def _():
        start_tile(posn_ref, 1 - slot)

    for k in range(2):
        lax.fori_loop(0, tc, lambda r, c: (row_copy(pos_ref, k, r, slot).wait(), c)[1], 0, unroll=8)

    w = w_ref[...]
    sub = lax.broadcasted_iota(jnp.int32, (LANES, LANES), 0)
    for c0 in range(0, tc, LANES):
        wsq = jnp.where(sub == 0, w[0:1, c0:c0 + LANES], jnp.where(sub == 1, w[1:2, c0:c0 + LANES], 0.0))
        wt = wsq.T
        y = wt[:, 0:1] * _tiles_to_rows(buf.at[slot, 0], c0, LANES) + \
            wt[:, 1:2] * _tiles_to_rows(buf.at[slot, 1], c0, LANES)
        o_ref[c0:c0 + LANES, :] = _layer_norm_rows(DN_ALPHA * x_ref[c0:c0 + LANES, :] + y,
                                                   lg_ref[...], lb_ref[...])


def _combine(ys, pos3, ew, x, lg, lb, tc):
    n_tok, d = x.shape
    n_t = n_tok // tc
    kern = functools.partial(_combine_kernel, tc=tc)
    vec = lambda i: (0, 0)
    return pl.pallas_call(
        kern,
        out_shape=jax.ShapeDtypeStruct((n_tok, d), F32),
        grid=(n_t,),
        in_specs=[pl.BlockSpec((None, 2, tc), lambda i: (i, 0, 0), memory_space=pltpu.SMEM),
                  pl.BlockSpec((None, 2, tc), lambda i: (jnp.minimum(i + 1, n_t - 1), 0, 0),
                               memory_space=pltpu.SMEM),
                  pl.BlockSpec((2, tc), lambda i: (0, i)),
                  pl.BlockSpec((tc, d), lambda i: (i, 0)),
                  pl.BlockSpec((1, d), vec),
                  pl.BlockSpec((1, d), vec),
                  pl.BlockSpec(memory_space=pl.ANY)],
        out_specs=pl.BlockSpec((tc, d), lambda i: (i, 0)),
        scratch_shapes=[pltpu.VMEM((2, 2, tc * SUBLANES, LANES), F32), pltpu.SemaphoreType.DMA((2,))],
        compiler_params=_cparams(("arbitrary",)),
        name="moe_combine",
    )(pos3, pos3, ew, x, lg.reshape(1, -1), lb.reshape(1, -1), ys)


def _moe(x, e_idx, ew, wg, wu, wd, layer, lg, lb):
    pos, plan = _moe_plan(e_idx, MOE_TILE)
    n_t = x.shape[0] // MOE_TILE
    pos3 = pos.reshape(2, n_t, MOE_TILE).transpose(1, 0, 2)
    xs = _dispatch(x, pos3, MOE_TILE)
    ys = _gmm(xs, plan, wg, wu, wd, layer, MOE_TILE)
    return _combine(ys, pos3, ew, x, lg, lb, MOE_TILE)


def kernel(x_prompt, x_sample, cache_k, cache_v, page_table, state_conv, state_ret, w_in_a, lambda_a, subln_a,
           conv_w, conv_b, conv_ln_g, conv_ln_b, w_out_a, w_in_c, w_out_c, ln_g, ln_b, w_router, b_router,
           w_e_gate, w_e_up, w_e_down):
    bsz, seq, _ = x_prompt.shape
    dec_b, dec_seq, _ = x_sample.shape
    n_pages = page_table.shape[1]
    past_len = n_pages * PAGE_SIZE
    tp = bsz * seq
    ts_ = dec_b * dec_seq
    tm = min(512, seq)
    tm_in = min(1024, seq)
    assert seq % tm_in == 0 and ts_ % tm_in == 0 and dec_seq == SUBLANES
    seq_tiles = seq // tm_in
    n_prompt_tiles = tp // tm_in

    pos_p = jnp.arange(seq, dtype=jnp.int32)
    pos_s = past_len + (jnp.arange(tm_in, dtype=jnp.int32) % dec_seq)
    pos_tab = jnp.concatenate([pos_p, pos_s])
    tabs_a = _rot_tables_a(pos_tab)
    tabs_c = _rot_tables_c(pos_tab)

    n_phys = cache_k.shape[1]
    ck = jnp.transpose(cache_k, (0, 1, 3, 4, 2))
    cv = cache_v.reshape(cache_v.shape[0], n_phys, PAGE_SIZE * H_A, D_VA)
    w_router_t = w_router.T
    w_in_a, w_out_a, w_in_c, w_out_c = (w.astype(BF16) for w in (w_in_a, w_out_a, w_in_c, w_out_c))

    x = jnp.concatenate([x_prompt.reshape(tp, D_MODEL), x_sample.reshape(ts_, D_MODEL)], axis=0)
    k_p, v_p, conv_p, ret_p, k_s, v_s, conv_s, ret_s = [], [], [], [], [], [], [], []
    for l in range(DEPTH):
        if l % 2 == 0:
            ia = l // 2
            lam_init = 0.8 - 0.6 * math.exp(-0.3 * l)
            qkvag = _inproj(x, w_in_a, ia, tabs_a, _inproj_a_kernel, tm_in, QK_W, seq_tiles, n_prompt_tiles,
                            "inproj_a")
            att_p = _attn_prompt(qkvag, lambda_a[ia], subln_a[ia], bsz, seq, lam_init, min(256, seq))
            att_s = _attn_sample(qkvag, ck, cv, ia, page_table, lambda_a[ia], subln_a[ia], tp, dec_b,
                                 dec_seq, lam_init)
            c_p, st_p = _conv_prompt(qkvag, conv_w[ia], conv_b[ia], conv_ln_g[ia], conv_ln_b[ia], bsz, seq,
                                     min(256, seq))
            c_s, st_s = _conv_sample(qkvag, state_conv, ia, conv_w[ia], conv_b[ia], conv_ln_g[ia],
                                     conv_ln_b[ia], tp, dec_b, dec_seq, 8)
            x, e_idx, ew = _outproj([att_p, c_p], [att_s, c_s], x, w_out_a, ia, ln_g[l, 0], ln_b[l, 0], w_router_t,
                                    b_router, tm, "outproj_a")
            k_all = qkvag[:, QK_W:2 * QK_W]
            v_all = qkvag[:, 2 * QK_W:2 * QK_W + ATT_W]
            k_p.append(k_all[:tp].reshape(bsz, seq, 2 * H_A, D_QK))
            v_p.append(v_all[:tp].reshape(bsz, seq, H_A, D_VA))
            k_s.append(k_all[tp:].reshape(dec_b, dec_seq, 2 * H_A, D_QK))
            v_s.append(v_all[tp:].reshape(dec_b, dec_seq, H_A, D_VA))
            conv_p.append(st_p)
            conv_s.append(st_s)
        else:
            ic = l // 2
            qkvg = _inproj(x, w_in_c, ic, tabs_c, _inproj_c_kernel, tm_in, H_C * DK_C, seq_tiles, n_prompt_tiles,
                           "inproj_c")
            og_p, s_p = _ret_prompt(qkvg, bsz, seq, min(256, seq))
            og_s, s_s = _ret_sample(qkvg, state_ret, ic, tp, dec_b, dec_seq)
            x, e_idx, ew = _outproj([og_p], [og_s], x, w_out_c, ic, ln_g[l, 0], ln_b[l, 0], w_router_t, b_router,
                                    tm, "outproj_c")
            ret_p.append(s_p)
            ret_s.append(s_s)
        x = _moe(x, e_idx, ew, w_e_gate, w_e_up, w_e_down, l, ln_g[l, 1], ln_b[l, 1])

    y_prompt = x[:tp].reshape(bsz, seq, D_MODEL)
    y_sample = x[tp:].reshape(dec_b, dec_seq, D_MODEL)
    return (y_prompt, y_sample, jnp.stack(k_p), jnp.stack(v_p), jnp.stack(conv_p), jnp.stack(ret_p),
            jnp.stack(k_s), jnp.stack(v_s), jnp.stack(conv_s), jnp.stack(ret_s))
```

```python
import functools
import math

import jax
import jax.numpy as jnp
from jax import lax
from jax.experimental import pallas as pl
from jax.experimental.pallas import tpu as pltpu

F32 = jnp.float32
BF16 = jnp.bfloat16

D_MODEL = 1024
DEPTH = 4
PAGE_SIZE = 128
H_A = 4
D_QK = 64
D_VA = 2 * D_QK
QK_W = H_A * 2 * D_QK
ATT_W = H_A * D_VA
ROT_DIM = D_QK // 4
ROPE_THETA = 500000.0
CONV_CH = D_MODEL // 2
CONV_W = 31
H_C = 4
DK_C = D_MODEL // H_C
DV_C = 2 * DK_C
RET_THETA = 10000.0
N_EXPERTS = 16
N_GROUPS = 4
EXPERTS_PER_GROUP = N_EXPERTS // N_GROUPS
D_FF = D_MODEL // 2
DN_ALPHA = (2 * DEPTH) ** 0.25
LN_EPS = 1e-5
IN_A = 2 * QK_W + ATT_W + 2 * CONV_CH
IN_C = 2 * H_C * DK_C + 2 * H_C * DV_C

LANES = 128
SUBLANES = 8
CONV_HALO = 32
VMEM_LIMIT = 56 * 1024 * 1024
MOE_TILE = 256
DMA_UNROLL = 8


def _cparams(sem):
    return pltpu.CompilerParams(dimension_semantics=sem, vmem_limit_bytes=VMEM_LIMIT)


def _bdot(a, b):
    return jnp.dot(a.astype(BF16), b.astype(BF16), preferred_element_type=F32)


def _bdot_nt(a, b):
    return lax.dot_general(a.astype(BF16), b.astype(BF16), (((1,), (1,)), ((), ())),
                           preferred_element_type=F32)


def _layer_norm_rows(v, g, b):
    mu = jnp.mean(v, -1, keepdims=True)
    d = v - mu
    var = jnp.mean(d * d, -1, keepdims=True)
    return d * lax.rsqrt(var + LN_EPS) * g + b


def _silu(v):
    return v * (1.0 / (1.0 + jnp.exp(-v)))


def _sigmoid(v):
    return 1.0 / (1.0 + jnp.exp(-v))


def _cast_rows_once(x_ref, xb_ref):
    @pl.when(pl.program_id(1) == 0)
    def _():
        xb_ref[...] = x_ref[...].astype(BF16)


def _inproj_a_kernel(x_ref, w_ref, cos_ref, sa_ref, sb_ref, o_ref, xb_ref):
    j = pl.program_id(1)
    _cast_rows_once(x_ref, xb_ref)
    y = jnp.dot(xb_ref[...], w_ref[...], preferred_element_type=F32)
    tn = y.shape[1]
    rot_w = 2 * QK_W
    rot_tiles = max(rot_w // tn, 1)
    rot_cols = min(rot_w, tn)
    assert rot_tiles * rot_cols == rot_w

    @pl.when(j < rot_tiles)
    def _():
        cos = cos_ref[...]
        sa = sa_ref[...]
        sb = sb_ref[...]
        for blk in range(rot_cols // LANES):
            t = y[:, blk * LANES:(blk + 1) * LANES]
            r = (t * cos + pltpu.roll(t, ROT_DIM // 2, 1) * sa
                 + pltpu.roll(t, LANES - ROT_DIM // 2, 1) * sb)
            o_ref[:, blk * LANES:(blk + 1) * LANES] = r
        if rot_cols < tn:
            o_ref[:, rot_cols:] = y[:, rot_cols:]

    @pl.when(j >= rot_tiles)
    def _():
        o_ref[...] = y


def _rot_tables_a(pos):
    half = ROT_DIM // 2
    inv = ROPE_THETA ** (-jnp.arange(0, ROT_DIM, 2, dtype=F32) / ROT_DIM)
    ang = pos.astype(F32)[:, None] * inv[None, :]
    c, s = jnp.cos(ang), jnp.sin(ang)
    n = pos.shape[0]
    one = jnp.ones((n, D_QK - ROT_DIM), F32)
    zero = jnp.zeros((n, D_QK - ROT_DIM), F32)
    zh = jnp.zeros((n, half), F32)
    cos64 = jnp.concatenate([c, c, one], 1)
    sa64 = jnp.concatenate([zh, s, zero], 1)
    sb64 = jnp.concatenate([-s, zh, zero], 1)
    rep = LANES // D_QK
    return jnp.tile(cos64, (1, rep)), jnp.tile(sa64, (1, rep)), jnp.tile(sb64, (1, rep))


def _inproj_c_kernel(x_ref, w_ref, cos_ref, sin_ref, o_ref, xb_ref):
    j = pl.program_id(1)
    _cast_rows_once(x_ref, xb_ref)
    y = jnp.dot(xb_ref[...], w_ref[...], preferred_element_type=F32)
    assert y.shape[1] == H_C * DK_C

    @pl.when(j < 2)
    def _():
        cos = cos_ref[...]
        sin = sin_ref[...]
        scale = jnp.where(j == 1, DK_C ** -0.5, 1.0).astype(F32)
        half = DK_C // 2
        for hd in range(H_C):
            t1 = y[:, hd * DK_C:hd * DK_C + half]
            t2 = y[:, hd * DK_C + half:(hd + 1) * DK_C]
            o_ref[:, hd * DK_C:hd * DK_C + half] = (t1 * cos - t2 * sin) * scale
            o_ref[:, hd * DK_C + half:(hd + 1) * DK_C] = (t2 * cos + t1 * sin) * scale

    @pl.when(j >= 2)
    def _():
        o_ref[...] = y


def _rot_tables_c(pos):
    inv = RET_THETA ** (-jnp.linspace(0.0, 1.0, DK_C // 2, dtype=F32))
    ang = pos.astype(F32)[:, None] * inv[None, :]
    return jnp.cos(ang), jnp.sin(ang)


def _inproj(x, w, li, tables, kern, tm, tn, seq_tiles, n_prompt_tiles, name):
    t_all, d = x.shape
    n = w.shape[2]
    assert n % tn == 0
    tw = tables[0].shape[1]

    def tab_map(i, j):
        return (jnp.where(i < n_prompt_tiles, i % seq_tiles, seq_tiles), 0)

    return pl.pallas_call(
        kern,
        out_shape=jax.ShapeDtypeStruct((t_all, n), F32),
        grid=(t_all // tm, n // tn),
        in_specs=[pl.BlockSpec((tm, d), lambda i, j: (i, 0)),
                  pl.BlockSpec((None, d, tn), lambda i, j: (li, 0, j))]
                 + [pl.BlockSpec((tm, tw), tab_map) for _ in tables],
        out_specs=pl.BlockSpec((tm, tn), lambda i, j: (i, j)),
        scratch_shapes=[pltpu.VMEM((tm, d), BF16)],
        compiler_params=_cparams(("parallel", "arbitrary")),
        name=name,
    )(x, w, *tables)


def _lambda_value(lam_ref, lam_init):
    lf = lam_ref[...]
    a = jnp.sum(lf[0:1, :] * lf[1:2, :], axis=-1, keepdims=True)
    b = jnp.sum(lf[2:3, :] * lf[3:4, :], axis=-1, keepdims=True)
    return jnp.exp(a) - jnp.exp(b) + lam_init


def _sub_norm(o, g, lam_init):
    ms = jnp.mean(o * o, -1, keepdims=True)
    return o * lax.rsqrt(ms + LN_EPS) * g * (1.0 - lam_init)


def _attn_prompt_kernel(q_ref, k_ref, v_ref, lam_ref, g_ref, o_ref, qt_scr, kb_scr, vt_scr, m_scr, acc_scr,
                        *, lam_init, tq):
    qi = pl.program_id(1)
    n_chain = 2 * H_A
    n_kt = k_ref.shape[0] // tq

    @pl.when(qi == 0)
    def _():
        kb_scr[...] = k_ref[...].astype(BF16)
        ones = jnp.ones((D_VA, tq), BF16)
        for j in range(n_kt):
            for h in range(H_A):
                vt_scr[j, h, 0:D_VA, :] = v_ref[j * tq:(j + 1) * tq, h * D_VA:(h + 1) * D_VA].T.astype(BF16)
                vt_scr[j, h, D_VA:2 * D_VA, :] = ones

    sub = lax.broadcasted_iota(jnp.int32, (D_VA, tq), 0)
    for h in range(H_A):
        qht = (q_ref[:, h * D_VA:(h + 1) * D_VA] * (D_QK ** -0.5)).T
        for m in range(2):
            qt_scr[2 * h + m] = jnp.where(sub // D_QK == m, qht, 0.0).astype(BF16)
    m_scr[...] = jnp.full(m_scr.shape, -jnp.inf, F32)
    acc_scr[...] = jnp.zeros(acc_scr.shape, F32)
    key = lax.broadcasted_iota(jnp.int32, (tq, tq), 0)
    qry = lax.broadcasted_iota(jnp.int32, (tq, tq), 1)

    def block(j, diagonal):
        start = pl.multiple_of(j * tq, tq)
        for c in range(n_chain):
            h = c // 2
            st = jnp.dot(kb_scr[pl.ds(start, tq), h * D_VA:(h + 1) * D_VA], qt_scr[c],
                         preferred_element_type=F32)
            if diagonal:
                st = jnp.where(key <= qry, st, -jnp.inf)
            m_prev = m_scr[c]
            m_new = jnp.maximum(m_prev, jnp.max(st, 0, keepdims=True))
            pt = jnp.exp(st - m_new).astype(BF16)
            pv = jnp.dot(vt_scr[j, h], pt, preferred_element_type=F32)
            acc_scr[c] = jnp.exp(m_prev - m_new) * acc_scr[c] + pv
            m_scr[c] = m_new

    def trip(j, carry):
        block(j, False)
        return carry

    lax.fori_loop(0, qi, trip, 0)
    block(qi, True)
    lam = _lambda_value(lam_ref, lam_init)
    for h in range(H_A):
        a0 = acc_scr[2 * h]
        a1 = acc_scr[2 * h + 1]
        ot = a0[:D_VA] / a0[D_VA:D_VA + 1] - lam * (a1[:D_VA] / a1[D_VA:D_VA + 1])
        o_ref[:, h * D_VA:(h + 1) * D_VA] = _sub_norm(ot.T, g_ref[...], lam_init)


def _attn_prompt(qkvag, lam_p, subln_g, bsz, seq, lam_init, tq):
    nq = seq // tq
    kern = functools.partial(_attn_prompt_kernel, lam_init=lam_init, tq=tq)
    return pl.pallas_call(
        kern,
        out_shape=jax.ShapeDtypeStruct((bsz * seq, ATT_W), F32),
        grid=(bsz, nq),
        in_specs=[pl.BlockSpec((tq, QK_W), lambda b, qi: (b * nq + qi, 0)),
                  pl.BlockSpec((seq, QK_W), lambda b, qi: (b, 1)),
                  pl.BlockSpec((seq, ATT_W), lambda b, qi: (b, 2)),
                  pl.BlockSpec((4, D_QK), lambda b, qi: (0, 0)),
                  pl.BlockSpec((1, D_VA), lambda b, qi: (0, 0))],
        out_specs=pl.BlockSpec((tq, ATT_W), lambda b, qi: (b * nq + qi, 0)),
        scratch_shapes=[pltpu.VMEM((2 * H_A, D_VA, tq), BF16),
                        pltpu.VMEM((seq, QK_W), BF16),
                        pltpu.VMEM((nq, H_A, 2 * D_VA, tq), BF16),
                        pltpu.VMEM((2 * H_A, 1, tq), F32),
                        pltpu.VMEM((2 * H_A, 2 * D_VA, tq), F32)],
        compiler_params=_cparams(("parallel", "arbitrary")),
        name="attn_prompt",
    )(qkvag, qkvag, qkvag, lam_p, subln_g.reshape(1, D_VA))


def _attn_sample_kernel(pt_ref, q_ref, kn_ref, vn_ref, lam_ref, g_ref, *rest, lam_init, n_pages, dec_seq):
    k_refs = rest[:n_pages]
    v_refs = rest[n_pages:2 * n_pages]
    o_ref = rest[2 * n_pages]
    nrow = 2 * H_A * dec_seq
    q = q_ref[...] * (D_QK ** -0.5)
    qt = jnp.concatenate([q] * (2 * H_A), axis=0)
    rid = lax.broadcasted_iota(jnp.int32, (nrow, QK_W), 0)
    cid = lax.broadcasted_iota(jnp.int32, (nrow, QK_W), 1)
    qbd = jnp.where(cid // D_QK == rid // dec_seq, qt, 0.0).astype(BF16)

    s_past = [_bdot(qbd, kr[...].reshape(QK_W, PAGE_SIZE)) for kr in k_refs]
    s_new = _bdot_nt(qbd, kn_ref[...])
    qpos = lax.broadcasted_iota(jnp.int32, (nrow, dec_seq), 0) % dec_seq
    kpos = lax.broadcasted_iota(jnp.int32, (nrow, dec_seq), 1)
    s_new = jnp.where(kpos <= qpos, s_new, -jnp.inf)
    m = jnp.max(s_new, -1, keepdims=True)
    for s in s_past:
        m = jnp.maximum(m, jnp.max(s, -1, keepdims=True))
    p_new = jnp.exp(s_new - m)
    l = jnp.sum(p_new, -1, keepdims=True)
    p_past = []
    for s in s_past:
        p = jnp.exp(s - m)
        l = l + jnp.sum(p, -1, keepdims=True)
        p_past.append(p.astype(BF16))
    lam = _lambda_value(lam_ref, lam_init)
    g = g_ref[...]
    grp = 2 * dec_seq
    for h in range(H_A):
        r0 = h * grp
        acc = _bdot(p_new[r0:r0 + grp], vn_ref[:, h * D_VA:(h + 1) * D_VA])
        for p, vr in zip(p_past, v_refs):
            acc = acc + _bdot(p[r0:r0 + grp], vr[pl.ds(h, PAGE_SIZE, stride=H_A), :])
        acc = acc / l[r0:r0 + grp]
        o = acc[:dec_seq] - lam * acc[dec_seq:]
        o_ref[:, h * D_VA:(h + 1) * D_VA] = _sub_norm(o, g, lam_init)


def _attn_sample(qkvag, cache_kt, cache_vr, ia, page_table, lam_p, subln_g, row0, dec_b, dec_seq, lam_init):
    n_pages = page_table.shape[1]
    kern = functools.partial(_attn_sample_kernel, lam_init=lam_init, n_pages=n_pages, dec_seq=dec_seq)
    blk0 = row0 // dec_seq

    def k_spec(p):
        return pl.BlockSpec((None, None, 2 * H_A, D_QK, PAGE_SIZE), lambda s, pt: (ia, pt[s, p], 0, 0, 0))

    def v_spec(p):
        return pl.BlockSpec((None, None, PAGE_SIZE * H_A, D_VA), lambda s, pt: (ia, pt[s, p], 0, 0))

    grid_spec = pltpu.PrefetchScalarGridSpec(
        num_scalar_prefetch=1,
        grid=(dec_b,),
        in_specs=[pl.BlockSpec((dec_seq, QK_W), lambda s, pt: (blk0 + s, 0)),
                  pl.BlockSpec((dec_seq, QK_W), lambda s, pt: (blk0 + s, 1)),
                  pl.BlockSpec((dec_seq, ATT_W), lambda s, pt: (blk0 + s, 2)),
                  pl.BlockSpec((4, D_QK), lambda s, pt: (0, 0)),
                  pl.BlockSpec((1, D_VA), lambda s, pt: (0, 0))]
                 + [k_spec(p) for p in range(n_pages)]
                 + [v_spec(p) for p in range(n_pages)],
        out_specs=pl.BlockSpec((dec_seq, ATT_W), lambda s, pt: (s, 0)),
    )
    return pl.pallas_call(
        kern,
        out_shape=jax.ShapeDtypeStruct((dec_b * dec_seq, ATT_W), F32),
        grid_spec=grid_spec,
        compiler_params=_cparams(("arbitrary",)),
        name="attn_sample",
    )(page_table, qkvag, qkvag, qkvag, lam_p, subln_g.reshape(1, D_VA),
      *([cache_kt] * n_pages), *([cache_vr] * n_pages))


def _conv_taps(scr, cw_ref, row0, rows):
    acc = None
    for w in range(CONV_W):
        term = scr[pl.ds(row0 + w, rows), :] * cw_ref[w:w + 1, :]
        acc = term if acc is None else acc + term
    return acc


def _conv_prompt_kernel(a_ref, g_ref, ah_ref, gh_ref, cw_ref, cb_ref, lg_ref, lb_ref,
                        c_ref, st_ref, scr, *, ts, chunk):
    i = pl.program_id(1)
    n = pl.num_programs(1)
    u = a_ref[...] * _sigmoid(g_ref[...])
    uh = ah_ref[...] * _sigmoid(gh_ref[...])
    uh = jnp.where(i > 0, uh, 0.0)
    scr[0:CONV_HALO, :] = uh
    scr[CONV_HALO:CONV_HALO + ts, :] = u
    off = CONV_HALO - (CONV_W - 1)
    for c0 in range(0, ts, chunk):
        c = _conv_taps(scr, cw_ref, c0 + off, chunk) + cb_ref[...]
        c_ref[c0:c0 + chunk, :] = _silu(_layer_norm_rows(c, lg_ref[...], lb_ref[...]))

    @pl.when(i == n - 1)
    def _():
        st_ref[...] = scr[CONV_HALO + ts - (CONV_W - 1):CONV_HALO + ts, :]


def _conv_prompt(qkvag, cw, cb, lg, lb, bsz, seq, ts):
    ns = seq // ts
    hb = ts // CONV_HALO
    acol = (2 * QK_W + ATT_W) // CONV_CH
    kern = functools.partial(_conv_prompt_kernel, ts=ts, chunk=32)
    vec = lambda b, i: (0, 0)
    return pl.pallas_call(
        kern,
        out_shape=(jax.ShapeDtypeStruct((bsz * seq, CONV_CH), F32),
                   jax.ShapeDtypeStruct((bsz, CONV_W - 1, CONV_CH), F32)),
        grid=(bsz, ns),
        in_specs=[pl.BlockSpec((ts, CONV_CH), lambda b, i: (b * ns + i, acol)),
                  pl.BlockSpec((ts, CONV_CH), lambda b, i: (b * ns + i, acol + 1)),
                  pl.BlockSpec((CONV_HALO, CONV_CH), lambda b, i: (jnp.maximum((b * ns + i) * hb - 1, 0), acol)),
                  pl.BlockSpec((CONV_HALO, CONV_CH), lambda b, i: (jnp.maximum((b * ns + i) * hb - 1, 0), acol + 1)),
                  pl.BlockSpec((CONV_W, CONV_CH), vec),
                  pl.BlockSpec((1, CONV_CH), vec),
                  pl.BlockSpec((1, CONV_CH), vec),
                  pl.BlockSpec((1, CONV_CH), vec)],
        out_specs=(pl.BlockSpec((ts, CONV_CH), lambda b, i: (b * ns + i, 0)),
                   pl.BlockSpec((None, CONV_W - 1, CONV_CH), lambda b, i: (b, 0, 0))),
        scratch_shapes=[pltpu.VMEM((CONV_HALO + ts, CONV_CH), F32)],
        compiler_params=_cparams(("parallel", "arbitrary")),
        name="conv_prompt",
    )(qkvag, qkvag, qkvag, qkvag, cw, cb.reshape(1, -1), lg.reshape(1, -1), lb.reshape(1, -1))


def _conv_sample_kernel(a_ref, g_ref, st_ref, cw_ref, cb_ref, lg_ref, lb_ref, c_ref, so_ref, scr, *, nb, dec_seq):
    hist = CONV_W - 1
    u = a_ref[...] * _sigmoid(g_ref[...])
    for s in range(nb):
        scr[0:hist, :] = st_ref[s]
        scr[hist:hist + dec_seq, :] = u[s * dec_seq:(s + 1) * dec_seq, :]
        c = _conv_taps(scr, cw_ref, 0, dec_seq) + cb_ref[...]
        c_ref[s * dec_seq:(s + 1) * dec_seq, :] = _silu(_layer_norm_rows(c, lg_ref[...], lb_ref[...]))
        so_ref[s] = scr[dec_seq:dec_seq + hist, :]


def _conv_sample(qkvag, state, ia, cw, cb, lg, lb, row0, dec_b, dec_seq, nb):
    acol = (2 * QK_W + ATT_W) // CONV_CH
    blk0 = row0 // (nb * dec_seq)
    kern = functools.partial(_conv_sample_kernel, nb=nb, dec_seq=dec_seq)
    vec = lambda i: (0, 0)
    return pl.pallas_call(
        kern,
        out_shape=(jax.ShapeDtypeStruct((dec_b * dec_seq, CONV_CH), F32),
                   jax.ShapeDtypeStruct((dec_b, CONV_W - 1, CONV_CH), F32)),
        grid=(dec_b // nb,),
        in_specs=[pl.BlockSpec((nb * dec_seq, CONV_CH), lambda i: (blk0 + i, acol)),
                  pl.BlockSpec((nb * dec_seq, CONV_CH), lambda i: (blk0 + i, acol + 1)),
                  pl.BlockSpec((None, nb, CONV_W - 1, CONV_CH), lambda i: (ia, i, 0, 0)),
                  pl.BlockSpec((CONV_W, CONV_CH), vec),
                  pl.BlockSpec((1, CONV_CH), vec),
                  pl.BlockSpec((1, CONV_CH), vec),
                  pl.BlockSpec((1, CONV_CH), vec)],
        out_specs=(pl.BlockSpec((nb * dec_seq, CONV_CH), lambda i: (i, 0)),
                   pl.BlockSpec((nb, CONV_W - 1, CONV_CH), lambda i: (i, 0, 0))),
        scratch_shapes=[pltpu.VMEM((CONV_W - 1 + dec_seq + SUBLANES, CONV_CH), F32)],
        compiler_params=_cparams(("arbitrary",)),
        name="conv_sample",
    )(qkvag, qkvag, state, cw, cb.reshape(1, -1), lg.reshape(1, -1), lb.reshape(1, -1))


def _ret_tables(c):
    log_g = jnp.log1p(-jnp.exp2(-5.0 - jnp.arange(H_C, dtype=F32)))
    idx = jnp.arange(c, dtype=F32)
    diff = idx[:, None] - idx[None, :]
    causal = diff >= 0
    d_intra = jnp.where(causal[None], jnp.exp(jnp.where(causal, diff, 0.0)[None] * log_g[:, None, None]), 0.0)
    q_dec = jnp.exp((idx[None, :] + 1.0) * log_g[:, None])[..., None]
    k_dec = jnp.exp((c - 1.0 - idx[None, :]) * log_g[:, None])[..., None]
    c_dec = jnp.exp(c * log_g)[:, None, None]
    return d_intra, q_dec, k_dec, c_dec


def _head_norm_gate(o, g):
    mu = jnp.mean(o, -1, keepdims=True)
    d = o - mu
    var = jnp.mean(d * d, -1, keepdims=True)
    return _silu(g) * (d * lax.rsqrt(var + LN_EPS))


def _ret_step(q, k, v, s, d_intra, q_dec, k_dec, c_dec):
    vb = v.astype(BF16)
    sc = _bdot_nt(q, k) * d_intra
    o = jnp.dot(sc.astype(BF16), vb, preferred_element_type=F32) + _bdot(q * q_dec, s)
    s_new = s * c_dec + jnp.dot((k * k_dec).T.astype(BF16), vb, preferred_element_type=F32)
    return o, s_new


def _ret_prompt_kernel(q_ref, k_ref, v_ref, g_ref, di_ref, qd_ref, kd_ref, cd_ref, o_ref, so_ref, s_scr):
    ci = pl.program_id(1)

    @pl.when(ci == 0)
    def _():
        s_scr[...] = jnp.zeros(s_scr.shape, F32)

    for h in range(H_C):
        o, s_new = _ret_step(q_ref[:, h * DK_C:(h + 1) * DK_C], k_ref[:, h * DK_C:(h + 1) * DK_C],
                             v_ref[:, h * DV_C:(h + 1) * DV_C], s_scr[h],
                             di_ref[h], qd_ref[h], kd_ref[h], cd_ref[h])
        s_scr[h] = s_new
        o_ref[:, h * DV_C:(h + 1) * DV_C] = _head_norm_gate(o, g_ref[:, h * DV_C:(h + 1) * DV_C])

    @pl.when(ci == pl.num_programs(1) - 1)
    def _():
        so_ref[...] = s_scr[...]


def _ret_prompt(qkvg, bsz, seq, chunk):
    nc = seq // chunk
    di, qd, kd, cd = _ret_tables(chunk)
    qw = H_C * DK_C
    vw = H_C * DV_C
    whole3 = lambda b, c: (0, 0, 0)
    return pl.pallas_call(
        _ret_prompt_kernel,
        out_shape=(jax.ShapeDtypeStruct((bsz * seq, vw), F32),
                   jax.ShapeDtypeStruct((bsz, H_C, DK_C, DV_C), F32)),
        grid=(bsz, nc),
        in_specs=[pl.BlockSpec((chunk, qw), lambda b, c: (b * nc + c, 0)),
                  pl.BlockSpec((chunk, qw), lambda b, c: (b * nc + c, 1)),
                  pl.BlockSpec((chunk, vw), lambda b, c: (b * nc + c, 1)),
                  pl.BlockSpec((chunk, vw), lambda b, c: (b * nc + c, 2)),
                  pl.BlockSpec((H_C, chunk, chunk), whole3),
                  pl.BlockSpec((H_C, chunk, 1), whole3),
                  pl.BlockSpec((H_C, chunk, 1), whole3),
                  pl.BlockSpec((H_C, 1, 1), whole3)],
        out_specs=(pl.BlockSpec((chunk, vw), lambda b, c: (b * nc + c, 0)),
                   pl.BlockSpec((None, H_C, DK_C, DV_C), lambda b, c: (b, 0, 0, 0))),
        scratch_shapes=[pltpu.VMEM((H_C, DK_C, DV_C), F32)],
        compiler_params=_cparams(("parallel", "arbitrary")),
        name="ret_prompt",
    )(qkvg, qkvg, qkvg, qkvg, di, qd, kd, cd)


def _ret_sample_kernel(q_ref, k_ref, v_ref, g_ref, s_ref, di_ref, qd_ref, kd_ref, cd_ref, o_ref, so_ref):
    for h in range(H_C):
        o, s_new = _ret_step(q_ref[:, h * DK_C:(h + 1) * DK_C], k_ref[:, h * DK_C:(h + 1) * DK_C],
                             v_ref[:, h * DV_C:(h + 1) * DV_C], s_ref[h],
                             di_ref[h], qd_ref[h], kd_ref[h], cd_ref[h])
        so_ref[h] = s_new
        o_ref[:, h * DV_C:(h + 1) * DV_C] = _head_norm_gate(o, g_ref[:, h * DV_C:(h + 1) * DV_C])


def _ret_sample(qkvg, state, ic, row0, dec_b, dec_seq):
    di, qd, kd, cd = _ret_tables(dec_seq)
    blk0 = row0 // dec_seq
    qw = H_C * DK_C
    vw = H_C * DV_C
    whole3 = lambda s: (0, 0, 0)
    return pl.pallas_call(
        _ret_sample_kernel,
        out_shape=(jax.ShapeDtypeStruct((dec_b * dec_seq, vw), F32),
                   jax.ShapeDtypeStruct((dec_b, H_C, DK_C, DV_C), F32)),
        grid=(dec_b,),
        in_specs=[pl.BlockSpec((dec_seq, qw), lambda s: (blk0 + s, 0)),
                  pl.BlockSpec((dec_seq, qw), lambda s: (blk0 + s, 1)),
                  pl.BlockSpec((dec_seq, vw), lambda s: (blk0 + s, 1)),
                  pl.BlockSpec((dec_seq, vw), lambda s: (blk0 + s, 2)),
                  pl.BlockSpec((None, None, H_C, DK_C, DV_C), lambda s: (ic, s, 0, 0, 0)),
                  pl.BlockSpec((H_C, dec_seq, dec_seq), whole3),
                  pl.BlockSpec((H_C, dec_seq, 1), whole3),
                  pl.BlockSpec((H_C, dec_seq, 1), whole3),
                  pl.BlockSpec((H_C, 1, 1), whole3)],
        out_specs=(pl.BlockSpec((dec_seq, vw), lambda s: (s, 0)),
                   pl.BlockSpec((None, H_C, DK_C, DV_C), lambda s: (s, 0, 0, 0))),
        compiler_params=_cparams(("arbitrary",)),
        name="ret_sample",
    )(qkvg, qkvg, qkvg, qkvg, state, di, qd, kd, cd)


def _route(logits):
    rows = [logits[e:e + 1, :] for e in range(N_EXPERTS)]
    mx = functools.reduce(jnp.maximum, rows)
    ex = [jnp.exp(r - mx) for r in rows]
    den = functools.reduce(lambda a, b: a + b, ex)
    probs = [e / den for e in ex]
    return probs


def _pick_top2(probs, bias_ref):
    sel = [probs[e] + bias_ref[e:e + 1, :] for e in range(N_EXPERTS)]
    epg = EXPERTS_PER_GROUP
    neg = jnp.full_like(sel[0], -jnp.inf)
    gscore = []
    for g in range(N_GROUPS):
        grp = sel[g * epg:(g + 1) * epg]
        pairs = [grp[a] + grp[b] for a in range(epg) for b in range(a + 1, epg)]
        gscore.append(functools.reduce(jnp.maximum, pairs))
    best = functools.reduce(jnp.maximum, gscore)
    g_idx = jnp.full(best.shape, N_GROUPS - 1, jnp.int32)
    for g in range(N_GROUPS - 2, -1, -1):
        g_idx = jnp.where(gscore[g] == best, g, g_idx)
    ing = []
    inp = []
    for j in range(epg):
        sv = sel[j]
        pv = probs[j]
        for g in range(1, N_GROUPS):
            sv = jnp.where(g_idx == g, sel[g * epg + j], sv)
            pv = jnp.where(g_idx == g, probs[g * epg + j], pv)
        ing.append(sv)
        inp.append(pv)
    top1 = functools.reduce(jnp.maximum, ing)
    l1 = jnp.full(best.shape, epg - 1, jnp.int32)
    for j in range(epg - 2, -1, -1):
        l1 = jnp.where(ing[j] == top1, j, l1)
    rest = [jnp.where(l1 == j, neg, ing[j]) for j in range(epg)]
    top2 = functools.reduce(jnp.maximum, rest)
    l2 = jnp.full(best.shape, epg - 1, jnp.int32)
    for j in range(epg - 2, -1, -1):
        l2 = jnp.where(jnp.logical_and(rest[j] == top2, l1 != j), j, l2)
    p1 = functools.reduce(lambda a, b: a + b, [jnp.where(l1 == j, inp[j], 0.0) for j in range(epg)])
    p2 = functools.reduce(lambda a, b: a + b, [jnp.where(l2 == j, inp[j], 0.0) for j in range(epg)])
    tot = p1 + p2
    return g_idx * epg + l1, g_idx * epg + l2, p1 / tot, p2 / tot


def _outproj_kernel(*refs, n_in, n_prompt_tiles):
    p_refs = refs[:n_in]
    s_refs = refs[n_in:2 * n_in]
    x_ref, w_ref, lg_ref, lb_ref, wr_ref, br_ref, o_ref, eidx_ref, ew_ref, a_scr = refs[2 * n_in:]
    i = pl.program_id(0)

    def stage(part_refs):
        col = 0
        for r in part_refs:
            a_scr[:, col:col + r.shape[1]] = r[...].astype(BF16)
            col += r.shape[1]

    @pl.when(i < n_prompt_tiles)
    def _():
        stage(p_refs)

    @pl.when(i >= n_prompt_tiles)
    def _():
        stage(s_refs)

    mix = jnp.dot(a_scr[...], w_ref[...], preferred_element_type=F32)
    x1 = _layer_norm_rows(DN_ALPHA * x_ref[...] + mix, lg_ref[...], lb_ref[...])
    o_ref[...] = x1
    logits = lax.dot_general(wr_ref[...], x1, (((1,), (1,)), ((), ())),
                             precision=lax.Precision.HIGHEST, preferred_element_type=F32)
    probs = _route(logits)
    e1, e2, w1, w2 = _pick_top2(probs, br_ref)
    eidx_ref[0:1, :] = e1
    eidx_ref[1:2, :] = e2
    ew_ref[0:1, :] = w1
    ew_ref[1:2, :] = w2


def _outproj(parts_p, parts_s, x, w, li, lg, lb, w_router_t, b_router, tm, name):
    t_all = x.shape[0]
    n_pt = parts_p[0].shape[0] // tm
    kern = functools.partial(_outproj_kernel, n_in=len(parts_p), n_prompt_tiles=n_pt)
    vec = lambda i: (0, 0)
    return pl.pallas_call(
        kern,
        out_shape=(jax.ShapeDtypeStruct((t_all, D_MODEL), F32),
                   jax.ShapeDtypeStruct((2, t_all), jnp.int32),
                   jax.ShapeDtypeStruct((2, t_all), F32)),
        grid=(t_all // tm,),
        in_specs=[pl.BlockSpec((tm, p.shape[1]), lambda i: (jnp.minimum(i, n_pt - 1), 0)) for p in parts_p]
                 + [pl.BlockSpec((tm, p.shape[1]), lambda i: (jnp.maximum(i - n_pt, 0), 0)) for p in parts_s]
                 + [pl.BlockSpec((tm, D_MODEL), lambda i: (i, 0)),
                    pl.BlockSpec((None,) + w.shape[1:], lambda i: (li, 0, 0)),
                    pl.BlockSpec((1, D_MODEL), vec),
                    pl.BlockSpec((1, D_MODEL), vec),
                    pl.BlockSpec((N_EXPERTS, D_MODEL), vec),
                    pl.BlockSpec((N_EXPERTS, 1), vec)],
        out_specs=(pl.BlockSpec((tm, D_MODEL), lambda i: (i, 0)),
                   pl.BlockSpec((2, tm), lambda i: (0, i)),
                   pl.BlockSpec((2, tm), lambda i: (0, i))),
        scratch_shapes=[pltpu.VMEM((tm, w.shape[1]), BF16)],
        compiler_params=_cparams(("parallel",)),
        name=name,
    )(*parts_p, *parts_s, x, w, lg.reshape(1, -1), lb.reshape(1, -1), w_router_t, b_router.reshape(-1, 1))


def _moe_plan(e_idx, tile):
    n_assign = e_idx.size
    e = e_idx.reshape(n_assign)
    onehot = (e[:, None] == jnp.arange(N_EXPERTS, dtype=jnp.int32)[None, :]).astype(jnp.int32)
    csum = jnp.cumsum(onehot, axis=0)
    counts = csum[-1]
    ends = jnp.cumsum(counts)
    starts = ends - counts
    pos = jnp.sum(onehot * (starts[None, :] + csum - 1), axis=1)
    n_tiles = n_assign // tile
    n_steps = n_tiles + N_EXPERTS - 1
    first_tile = starts // tile
    tiles_e = jnp.where(counts > 0, (ends + tile - 1) // tile - first_tile, 0)
    step_end = jnp.cumsum(tiles_e)
    step_start = step_end - tiles_e
    total = step_end[-1]
    sidx = jnp.arange(n_steps, dtype=jnp.int32)
    valid = sidx < total
    s_eff = jnp.minimum(sidx, total - 1)
    exp_s = jnp.sum((step_end[None, :] <= s_eff[:, None]).astype(jnp.int32), axis=1)
    exp_s = jnp.minimum(exp_s, N_EXPERTS - 1)
    tile_s = first_tile[exp_s] + (s_eff - step_start[exp_s])
    lo = jnp.maximum(starts[exp_s], tile_s * tile) - tile_s * tile
    hi = jnp.minimum(ends[exp_s], (tile_s + 1) * tile) - tile_s * tile
    prev_tile = jnp.concatenate([jnp.full((1,), -1, jnp.int32), tile_s[:-1]])
    first = jnp.logical_and(valid, tile_s != prev_tile)
    i32 = lambda v: v.astype(jnp.int32)
    return i32(pos), (i32(tile_s), exp_s, i32(lo), i32(hi), i32(first), i32(valid))


def _rows_to_tiles(tile_ref, rows):
    n = rows.shape[0]
    for j in range(rows.shape[1] // LANES):
        tile_ref[pl.ds(j, n, stride=SUBLANES), :] = rows[:, j * LANES:(j + 1) * LANES]


def _tiles_to_rows(tile_ref, start, n):
    return jnp.concatenate([tile_ref[pl.ds(start * SUBLANES + j, n, stride=SUBLANES), :]
                            for j in range(D_MODEL // LANES)], axis=1)


def _token_tile(ref, t):
    return ref.at[pl.ds(pl.multiple_of(t * SUBLANES, SUBLANES), SUBLANES), :]


def _start_tokens(make_copy, n):
    def group(g, c):
        for u in range(DMA_UNROLL):
            make_copy(g * DMA_UNROLL + u).start(priority=u % 2)
        return c

    lax.fori_loop(0, n // DMA_UNROLL, group, 0)


def _wait_tokens(make_copy, n):
    lax.fori_loop(0, n, lambda r, c: (make_copy(r).wait(), c)[1], 0, unroll=DMA_UNROLL)


def _dispatch_kernel(pos_ref, posp_ref, x_ref, xs_hbm, xt_scr, sem, *, tc):
    i = pl.program_id(0)
    n = pl.num_programs(0)
    slot = i % 2
    _rows_to_tiles(xt_scr.at[slot], x_ref[...])

    def row_copy(p_ref, k, r, sl):
        return pltpu.make_async_copy(_token_tile(xt_scr.at[sl], r), _token_tile(xs_hbm, p_ref[k, r]), sem.at[sl])

    def wait_tile(p_ref, sl):
        for k in range(2):
            _wait_tokens(lambda r: row_copy(p_ref, k, r, sl), tc)

    for k in range(2):
        _start_tokens(lambda r: row_copy(pos_ref, k, r, slot), tc)

    @pl.when(i > 0)
    def _():
        wait_tile(posp_ref, 1 - slot)

    @pl.when(i == n - 1)
    def _():
        wait_tile(pos_ref, slot)


def _dispatch(x, pos3, tc):
    n_tok, d = x.shape
    kern = functools.partial(_dispatch_kernel, tc=tc)
    return pl.pallas_call(
        kern,
        out_shape=jax.ShapeDtypeStruct((2 * n_tok * SUBLANES, LANES), x.dtype),
        grid=(n_tok // tc,),
        in_specs=[pl.BlockSpec((None, 2, tc), lambda i: (i, 0, 0), memory_space=pltpu.SMEM),
                  pl.BlockSpec((None, 2, tc), lambda i: (jnp.maximum(i - 1, 0), 0, 0), memory_space=pltpu.SMEM),
                  pl.BlockSpec((tc, d), lambda i: (i, 0))],
        out_specs=pl.BlockSpec(memory_space=pl.ANY),
        scratch_shapes=[pltpu.VMEM((2, tc * SUBLANES, LANES), x.dtype), pltpu.SemaphoreType.DMA((2,))],
        compiler_params=_cparams(("arbitrary",)),
        name="moe_dispatch",
    )(pos3, pos3, x)


def _gmm_kernel(tile_ref, exp_ref, lo_ref, hi_ref, first_ref, valid_ref, xs_ref, wg_ref, wu_ref, wd_ref,
                o_ref, wg_b, wu_b, wd_b, xb_scr, *, tile):
    s = pl.program_id(0)
    prev = jnp.maximum(s - 1, 0)
    new_expert = jnp.logical_or(s == 0, exp_ref[s] != exp_ref[prev])

    @pl.when(new_expert)
    def _():
        wg_b[...] = wg_ref[...].astype(BF16)
        wu_b[...] = wu_ref[...].astype(BF16)
        wd_b[...] = wd_ref[...].astype(BF16)

    @pl.when(valid_ref[s] == 1)
    def _():
        xb_scr[...] = _tiles_to_rows(xs_ref, 0, tile).astype(BF16)
        xb = xb_scr[...]
        hid = _silu(jnp.dot(xb, wg_b[...], preferred_element_type=F32)) * \
            jnp.dot(xb, wu_b[...], preferred_element_type=F32)
        y = jnp.dot(hid.astype(BF16), wd_b[...], preferred_element_type=F32)
        rows = lax.broadcasted_iota(jnp.int32, (tile, 1), 0)
        mine = jnp.logical_and(rows >= lo_ref[s], rows < hi_ref[s])

        @pl.when(first_ref[s] == 1)
        def _():
            _rows_to_tiles(o_ref, jnp.where(mine, y, 0.0))

        @pl.when(first_ref[s] == 0)
        def _():
            _rows_to_tiles(o_ref, jnp.where(mine, y, _tiles_to_rows(o_ref, 0, tile)))


def _gmm(xs, plan, wg, wu, wd, layer, tile):
    d = D_MODEL
    n_steps = plan[0].shape[0]
    kern = functools.partial(_gmm_kernel, tile=tile)
    grid_spec = pltpu.PrefetchScalarGridSpec(
        num_scalar_prefetch=len(plan),
        grid=(n_steps,),
        in_specs=[pl.BlockSpec((tile * SUBLANES, LANES), lambda s, t, e, *_: (t[s], 0)),
                  pl.BlockSpec((None, None, d, D_FF), lambda s, t, e, *_: (layer, e[s], 0, 0)),
                  pl.BlockSpec((None, None, d, D_FF), lambda s, t, e, *_: (layer, e[s], 0, 0)),
                  pl.BlockSpec((None, None, D_FF, d), lambda s, t, e, *_: (layer, e[s], 0, 0))],
        out_specs=pl.BlockSpec((tile * SUBLANES, LANES), lambda s, t, e, *_: (t[s], 0)),
        scratch_shapes=[pltpu.VMEM((d, D_FF), BF16), pltpu.VMEM((d, D_FF), BF16), pltpu.VMEM((D_FF, d), BF16),
                        pltpu.VMEM((tile, d), BF16)],
    )
    return pl.pallas_call(
        kern,
        out_shape=jax.ShapeDtypeStruct(xs.shape, F32),
        grid_spec=grid_spec,
        compiler_params=_cparams(("arbitrary",)),
        name="moe_gmm",
    )(*plan, xs, wg, wu, wd)


def _combine_kernel(pos_ref, posn_ref, w_ref, x_ref, lg_ref, lb_ref, ys_hbm, o_ref, buf, sem, *, tc):
    i = pl.program_id(0)
    n = pl.num_programs(0)
    slot = i % 2

    def row_copy(p_ref, k, r, sl):
        return pltpu.make_async_copy(_token_tile(ys_hbm, p_ref[k, r]), _token_tile(buf.at[sl, k], r), sem.at[sl])

    def start_tile(p_ref, sl):
        for k in range(2):
            _start_tokens(lambda r: row_copy(p_ref, k, r, sl), tc)

    @pl.when(i == 0)
    def _():
        start_tile(pos_ref, 0)

    @pl.when(i + 1 < n)
    def _():
        start_tile(posn_ref, 1 - slot)

    for k in range(2):
        _wait_tokens(lambda r: row_copy(pos_ref, k, r, slot), tc)

    w = w_ref[...]
    sub = lax.broadcasted_iota(jnp.int32, (LANES, LANES), 0)
    for c0 in range(0, tc, LANES):
        wsq = jnp.where(sub == 0, w[0:1, c0:c0 + LANES], jnp.where(sub == 1, w[1:2, c0:c0 + LANES], 0.0))
        wt = wsq.T
        y = wt[:, 0:1] * _tiles_to_rows(buf.at[slot, 0], c0, LANES) + \
            wt[:, 1:2] * _tiles_to_rows(buf.at[slot, 1], c0, LANES)
        o_ref[c0:c0 + LANES, :] = _layer_norm_rows(DN_ALPHA * x_ref[c0:c0 + LANES, :] + y,
                                                   lg_ref[...], lb_ref[...])


def _combine(ys, pos3, ew, x, lg, lb, tc):
    n_tok, d = x.shape
    n_t = n_tok // tc
    kern = functools.partial(_combine_kernel, tc=tc)
    vec = lambda i: (0, 0)
    return pl.pallas_call(
        kern,
        out_shape=jax.ShapeDtypeStruct((n_tok, d), F32),
        grid=(n_t,),
        in_specs=[pl.BlockSpec((None, 2, tc), lambda i: (i, 0, 0), memory_space=pltpu.SMEM),
                  pl.BlockSpec((None, 2, tc), lambda i: (jnp.minimum(i + 1, n_t - 1), 0, 0),
                               memory_space=pltpu.SMEM),
                  pl.BlockSpec((2, tc), lambda i: (0, i)),
                  pl.BlockSpec((tc, d), lambda i: (i, 0)),
                  pl.BlockSpec((1, d), vec),
                  pl.BlockSpec((1, d), vec),
                  pl.BlockSpec(memory_space=pl.ANY)],
        out_specs=pl.BlockSpec((tc, d), lambda i: (i, 0)),
        scratch_shapes=[pltpu.VMEM((2, 2, tc * SUBLANES, LANES), F32), pltpu.SemaphoreType.DMA((2,))],
        compiler_params=_cparams(("arbitrary",)),
        name="moe_combine",
    )(pos3, pos3, ew, x, lg.reshape(1, -1), lb.reshape(1, -1), ys)


def _moe(x, e_idx, ew, wg, wu, wd, layer, lg, lb):
    pos, plan = _moe_plan(e_idx, MOE_TILE)
    n_t = x.shape[0] // MOE_TILE
    pos3 = pos.reshape(2, n_t, MOE_TILE).transpose(1, 0, 2)
    xs = _dispatch(x, pos3, MOE_TILE)
    ys = _gmm(xs, plan, wg, wu, wd, layer, MOE_TILE)
    return _combine(ys, pos3, ew, x, lg, lb, MOE_TILE)


def kernel(x_prompt, x_sample, cache_k, cache_v, page_table, state_conv, state_ret, w_in_a, lambda_a, subln_a,
           conv_w, conv_b, conv_ln_g, conv_ln_b, w_out_a, w_in_c, w_out_c, ln_g, ln_b, w_router, b_router,
           w_e_gate, w_e_up, w_e_down):
    bsz, seq, _ = x_prompt.shape
    dec_b, dec_seq, _ = x_sample.shape
    n_pages = page_table.shape[1]
    past_len = n_pages * PAGE_SIZE
    tp = bsz * seq
    ts_ = dec_b * dec_seq
    tm = min(512, seq)
    tm_in = min(1024, seq)
    assert seq % tm_in == 0 and ts_ % tm_in == 0 and dec_seq == SUBLANES
    seq_tiles = seq // tm_in
    n_prompt_tiles = tp // tm_in

    pos_p = jnp.arange(seq, dtype=jnp.int32)
    pos_s = past_len + (jnp.arange(tm_in, dtype=jnp.int32) % dec_seq)
    pos_tab = jnp.concatenate([pos_p, pos_s])
    tabs_a = _rot_tables_a(pos_tab)
    tabs_c = _rot_tables_c(pos_tab)

    n_phys = cache_k.shape[1]
    ck = jnp.transpose(cache_k, (0, 1, 3, 4, 2))
    cv = cache_v.reshape(cache_v.shape[0], n_phys, PAGE_SIZE * H_A, D_VA)
    w_router_t = w_router.T
    w_in_a, w_out_a, w_in_c, w_out_c = (w.astype(BF16) for w in (w_in_a, w_out_a, w_in_c, w_out_c))

    x = jnp.concatenate([x_prompt.reshape(tp, D_MODEL), x_sample.reshape(ts_, D_MODEL)], axis=0)
    k_p, v_p, conv_p, ret_p, k_s, v_s, conv_s, ret_s = [], [], [], [], [], [], [], []
    for l in range(DEPTH):
        if l % 2 == 0:
            ia = l // 2
            lam_init = 0.8 - 0.6 * math.exp(-0.3 * l)
            qkvag = _inproj(x, w_in_a, ia, tabs_a, _inproj_a_kernel, tm_in, QK_W, seq_tiles, n_prompt_tiles,
                            "inproj_a")
            att_p = _attn_prompt(qkvag, lambda_a[ia], subln_a[ia], bsz, seq, lam_init, min(256, seq))
            att_s = _attn_sample(qkvag, ck, cv, ia, page_table, lambda_a[ia], subln_a[ia], tp, dec_b,
                                 dec_seq, lam_init)
            c_p, st_p = _conv_prompt(qkvag, conv_w[ia], conv_b[ia], conv_ln_g[ia], conv_ln_b[ia], bsz, seq,
                                     min(256, seq))
            c_s, st_s = _conv_sample(qkvag, state_conv, ia, conv_w[ia], conv_b[ia], conv_ln_g[ia],
                                     conv_ln_b[ia], tp, dec_b, dec_seq, 8)
            x, e_idx, ew = _outproj([att_p, c_p], [att_s, c_s], x, w_out_a, ia, ln_g[l, 0], ln_b[l, 0], w_router_t,
                                    b_router, tm, "outproj_a")
            k_all = qkvag[:, QK_W:2 * QK_W]
            v_all = qkvag[:, 2 * QK_W:2 * QK_W + ATT_W]
            k_p.append(k_all[:tp].reshape(bsz, seq, 2 * H_A, D_QK))
            v_p.append(v_all[:tp].reshape(bsz, seq, H_A, D_VA))
            k_s.append(k_all[tp:].reshape(dec_b, dec_seq, 2 * H_A, D_QK))
            v_s.append(v_all[tp:].reshape(dec_b, dec_seq, H_A, D_VA))
            conv_p.append(st_p)
            conv_s.append(st_s)
        else:
            ic = l // 2
            qkvg = _inproj(x, w_in_c, ic, tabs_c, _inproj_c_kernel, tm_in, H_C * DK_C, seq_tiles, n_prompt_tiles,
                           "inproj_c")
            og_p, s_p = _ret_prompt(qkvg, bsz, seq, min(256, seq))
            og_s, s_s = _ret_sample(qkvg, state_ret, ic, tp, dec_b, dec_seq)
            x, e_idx, ew = _outproj([og_p], [og_s], x, w_out_c, ic, ln_g[l, 0], ln_b[l, 0], w_router_t, b_router,
                                    tm, "outproj_c")
            ret_p.append(s_p)
            ret_s.append(s_s)
        x = _moe(x, e_idx, ew, w_e_gate, w_e_up, w_e_down, l, ln_g[l, 1], ln_b[l, 1])

    y_prompt = x[:tp].reshape(bsz, seq, D_MODEL)
    y_sample = x[tp:].reshape(dec_b, dec_seq, D_MODEL)
    return (y_prompt, y_sample, jnp.stack(k_p), jnp.stack(v_p), jnp.stack(conv_p), jnp.stack(ret_p),
            jnp.stack(k_s), jnp.stack(v_s), jnp.stack(conv_s), jnp.stack(ret_s))
```
